```python
import jax, jax.numpy as jnp
from jax import lax
import numpy as np

D_MODEL = 2048
BATCH = 16
SEQ = 256
DEPTH = 2
DEC_BATCH = 8
DEC_SEQ = 1024
PAST_LEN = 512

GRID_W = 64
N_EVEN = (DEPTH + 1) // 2
N_ODD = DEPTH // 2
N_MOD = 6
EPS = 1e-6
MLA_HEADS = 8
Q_LORA = 512
KV_LORA = 512
QK_NOPE = 128
QK_ROPE = 64
V_HEAD = 128
ROPE_BASE = 10000.0
QBLK = 128
CONV_DIM = D_MODEL - MLA_HEADS * V_HEAD
CONV_W = 3
EVEN_IN = Q_LORA + KV_LORA + QK_ROPE + 3 * CONV_DIM
ML_HEADS = 4
ML_DV = D_MODEL // ML_HEADS
ML_DK = ML_DV // 2
ML_IN = 2 * ML_HEADS * ML_DK + 2 * ML_HEADS * ML_DV + 4 * ML_HEADS
CHUNK = 64
FORGET_BIAS = 3.0
D_FF = 7 * D_MODEL // 2
N_EXPERTS = 8
TOP_K = 2

kernel_name = 'hybrid_mla_conv_mlstm_diffusion_step'


def rmsnorm(x, g):
    xf = x.astype(jnp.float32)
    y = xf * lax.rsqrt(jnp.mean(xf * xf, axis=-1, keepdims=True) + EPS)
    return (y * g.astype(jnp.float32)).astype(x.dtype)


def adaln(cond, w, b, dtype):
    m = (cond @ w.astype(jnp.float32) + b.astype(jnp.float32)).astype(dtype)
    return jnp.split(m[:, None, :], N_MOD, axis=-1)


def modulate(x, g, shift, scale):
    return rmsnorm(x, g) * (1 + scale) + shift


def axial_rope(T):
    rows = T // GRID_W
    row = jnp.repeat(jnp.arange(rows, dtype=jnp.float32), GRID_W)
    col = jnp.tile(jnp.arange(GRID_W, dtype=jnp.float32), rows)
    half = QK_ROPE // 2
    inv_freq = ROPE_BASE ** (-jnp.arange(0, half, 2, dtype=jnp.float32) / half)
    ang_r = row[:, None] * inv_freq
    ang_c = col[:, None] * inv_freq
    return (jnp.cos(ang_r), jnp.sin(ang_r), jnp.cos(ang_c), jnp.sin(ang_c))


def rot_half(x, cos, sin):
    x1, x2 = jnp.split(x, 2, axis=-1)
    return jnp.concatenate([x1 * cos - x2 * sin, x2 * cos + x1 * sin], axis=-1)


def axial_rotate(x, rope):
    cr, sr, cc, sc = rope
    extra = x.ndim - 3
    shp = lambda t: t.reshape(t.shape[:1] + (1,) * extra + t.shape[1:])
    xr, xc = jnp.split(x.astype(jnp.float32), 2, axis=-1)
    out = jnp.concatenate([rot_half(xr, shp(cr), shp(sr)), rot_half(xc, shp(cc), shp(sc))], axis=-1)
    return out.astype(x.dtype)


def block_attention(q, k, v):
    B, Tq, H, dq = q.shape
    nb = Tq // QBLK
    scale = dq ** -0.5
    qb = jnp.moveaxis(q.reshape(B, nb, QBLK, H, dq), 1, 0)

    def one_block(qi):
        s = jnp.einsum('bqhd,bkhd->bhqk', qi, k, preferred_element_type=jnp.float32) * scale
        p = jax.nn.softmax(s, axis=-1).astype(v.dtype)
        return jnp.einsum('bhqk,bkhe->bqhe', p, v)

    out = lax.map(one_block, qb)
    return jnp.moveaxis(out, 0, 1).reshape(B, Tq, H, v.shape[-1])


def short_conv(u, w):
    up = jnp.pad(u, ((0, 0), (1, 1), (0, 0)))
    return up[:, :-2] * w[0] + up[:, 1:-1] * w[1] + up[:, 2:] * w[2]


def mla_keys(ckv, kr, w_ukv):
    B, T, _ = ckv.shape
    kv = (ckv @ w_ukv).reshape(B, T, MLA_HEADS, QK_NOPE + V_HEAD)
    k_nope, v = kv[..., :QK_NOPE], kv[..., QK_NOPE:]
    k_rope = jnp.broadcast_to(kr[:, :, None, :], (B, T, MLA_HEADS, QK_ROPE))
    return jnp.concatenate([k_nope, k_rope], axis=-1), v


def even_mixer(h, rope, ctx_ckv, ctx_kr, w_in, g_q, g_kv, w_uq, w_ukv, conv_w, w_o):
    B, T, _ = h.shape
    c1 = Q_LORA
    c2 = c1 + KV_LORA
    c3 = c2 + QK_ROPE
    c4 = c3 + CONV_DIM
    c5 = c4 + CONV_DIM
    cq, ckv, kr, ux, ub, uc = jnp.split(h @ w_in, [c1, c2, c3, c4, c5], axis=-1)
    cq = rmsnorm(cq, g_q)
    ckv = rmsnorm(ckv, g_kv)
    q = (cq @ w_uq).reshape(B, T, MLA_HEADS, QK_NOPE + QK_ROPE)
    q_nope, q_rope = q[..., :QK_NOPE], q[..., QK_NOPE:]
    kr_pos = kr
    if rope is not None:
        q_rope = axial_rotate(q_rope, rope)
        kr_pos = axial_rotate(kr, rope)
    k, v = mla_keys(ckv, kr_pos, w_ukv)
    if ctx_ckv is not None:
        kc, vc = mla_keys(ctx_ckv, ctx_kr, w_ukv)
        k = jnp.concatenate([k, kc], axis=1)
        v = jnp.concatenate([v, vc], axis=1)
    att = block_attention(jnp.concatenate([q_nope, q_rope], axis=-1), k, v).reshape(B, T, MLA_HEADS * V_HEAD)
    conv = ub * short_conv(uc * ux, conv_w)
    out = jnp.concatenate([att, conv], axis=-1) @ w_o
    return out, ckv, kr


def mlstm_chunkwise(q, k, v, logi, logf, C0, n0, m0):
    B, T, H, _ = q.shape
    dv = v.shape[-1]
    nc = T // CHUNK

    def chunks(a):
        a = a.astype(jnp.float32).reshape((B, nc, CHUNK) + a.shape[2:])
        return jnp.swapaxes(jnp.moveaxis(a, 1, 0), 2, 3)

    causal = jnp.tril(jnp.ones((CHUNK, CHUNK), dtype=bool))

    def step(carry, xs):
        C, n, m = carry
        qc, kc, vc, ic, fc = xs
        b = jnp.cumsum(fc, axis=-1)
        dmat = jnp.where(causal, b[..., :, None] - b[..., None, :] + ic[..., None, :], -jnp.inf)
        a = b + m[..., None]
        mrow = jnp.maximum(a, jnp.max(dmat, axis=-1))
        w_intra = jnp.exp(dmat - mrow[..., None])
        w_inter = jnp.exp(a - mrow)
        s = jnp.einsum('bhjd,bhsd->bhjs', qc, kc) * w_intra
        num = w_inter[..., None] * jnp.einsum('bhjd,bhde->bhje', qc, C) + jnp.einsum('bhjs,bhse->bhje', s, vc)
        den = w_inter * jnp.einsum('bhjd,bhd->bhj', qc, n) + jnp.sum(s, axis=-1)
        hc = num / jnp.maximum(jnp.abs(den), jnp.exp(-mrow))[..., None]
        bl = b[..., -1]
        g = bl[..., None] - b + ic
        m_new = jnp.maximum(bl + m, jnp.max(g, axis=-1))
        wg = jnp.exp(g - m_new[..., None])
        decay = jnp.exp(bl + m - m_new)
        C_new = decay[..., None, None] * C + jnp.einsum('bhs,bhsd,bhse->bhde', wg, kc, vc)
        n_new = decay[..., None] * n + jnp.einsum('bhs,bhsd->bhd', wg, kc)
        return (C_new, n_new, m_new), hc

    init = (C0.astype(jnp.float32), n0.astype(jnp.float32), m0.astype(jnp.float32))
    (C, n, m), hs = lax.scan(step, init, (chunks(q), chunks(k), chunks(v), chunks(logi), chunks(logf)))
    hs = jnp.moveaxis(jnp.swapaxes(hs, 2, 3), 0, 1).reshape(B, T, H, dv)
    return hs, C, n, m


def odd_mixer(h, init, w_in, b_gates, g_h, w_o):
    B, T, _ = h.shape
    hk = ML_HEADS * ML_DK
    hv = ML_HEADS * ML_DV
    q, k, v, o, gates = jnp.split(h @ w_in, [hk, 2 * hk, 2 * hk + hv, 2 * hk + 2 * hv], axis=-1)
    q = q.reshape(B, T, ML_HEADS, ML_DK)
    k = k.reshape(B, T, ML_HEADS, ML_DK) * (ML_DK ** -0.5)
    v = v.reshape(B, T, ML_HEADS, ML_DV)
    g = (gates + b_gates).astype(jnp.float32).reshape(B, T, 4, ML_HEADS)
    logi_f, logf_f = g[:, :, 0], jax.nn.log_sigmoid(g[:, :, 1])
    logi_b, logf_b = g[:, :, 2], jax.nn.log_sigmoid(g[:, :, 3])
    if init is None:
        C0 = jnp.zeros((B, 2, ML_HEADS, ML_DK, ML_DV), jnp.float32)
        n0 = jnp.zeros((B, 2, ML_HEADS, ML_DK), jnp.float32)
        m0 = jnp.zeros((B, 2, ML_HEADS), jnp.float32)
    else:
        C0, n0, m0 = init
    h_f, Cf, nf, mf = mlstm_chunkwise(q, k, v, logi_f, logf_f, C0[:, 0], n0[:, 0], m0[:, 0])
    flip = lambda a: jnp.flip(a, axis=1)
    h_b, Cb, nb, mb = mlstm_chunkwise(flip(q), flip(k), flip(v), flip(logi_b), flip(logf_b), C0[:, 1], n0[:, 1], m0[:, 1])
    ht = rmsnorm(h_f + flip(h_b), g_h.reshape(ML_HEADS, ML_DV))
    y = jax.nn.sigmoid(o.astype(jnp.float32)) * ht.reshape(B, T, hv)
    out = y.astype(h.dtype) @ w_o
    return out, jnp.stack([Cf, Cb], axis=1), jnp.stack([nf, nb], axis=1), jnp.stack([mf, mb], axis=1)


def swiglu(h, w_gate, w_up, w_down):
    return (jax.nn.silu(h @ w_gate) * (h @ w_up)) @ w_down


def moe_ffn(h, w_router, w_gate, w_up, w_down):
    logits = (h @ w_router).astype(jnp.float32)
    top_v, top_i = lax.top_k(logits, TOP_K)
    probs = jax.nn.softmax(top_v, axis=-1)
    dense_gate = jnp.sum(jax.nn.one_hot(top_i, N_EXPERTS, dtype=jnp.float32) * probs[..., None], axis=-2).astype(h.dtype)
    out = jnp.zeros_like(h)
    for e in range(N_EXPERTS):
        out = out + dense_gate[..., e:e + 1] * swiglu(h, w_gate[e], w_up[e], w_down[e])
    return out


def setup_inputs(seed: int = 0) -> dict:
    key = jax.random.key(seed)
    ks = jax.random.split(key, 40)
    nrm = lambda i, shape, s: jax.random.normal(ks[i], shape, jnp.float32) * s
    D = D_MODEL
    gate_offset = jnp.repeat(jnp.array([0.0, FORGET_BIAS, 0.0, FORGET_BIAS], jnp.float32), ML_HEADS)
    return {
        'x_prompt': nrm(0, (BATCH, SEQ, D), 1.0),
        'x_sample': nrm(1, (DEC_BATCH, DEC_SEQ, D), 1.0),
        'c': nrm(2, (DEC_BATCH, D), 1.0),
        'cache_ckv': nrm(3, (DEC_BATCH, N_EVEN, PAST_LEN, KV_LORA), 1.0),
        'cache_krope': nrm(4, (DEC_BATCH, N_EVEN, PAST_LEN, QK_ROPE), 1.0),
        'state_C': nrm(5, (DEC_BATCH, N_ODD, 2, ML_HEADS, ML_DK, ML_DV), 0.1),
        'state_n': nrm(6, (DEC_BATCH, N_ODD, 2, ML_HEADS, ML_DK), 0.1),
        'state_m': nrm(7, (DEC_BATCH, N_ODD, 2, ML_HEADS), 0.5),
        'c_ctx': nrm(8, (D,), 1.0),
        'w_mod': nrm(9, (DEPTH, D, N_MOD * D), 0.5 * D ** -0.5),
        'b_mod': nrm(10, (DEPTH, N_MOD * D), 0.02),
        'g_mix': 1.0 + nrm(11, (DEPTH, D), 0.02),
        'g_ffn': 1.0 + nrm(12, (DEPTH, D), 0.02),
        'w_in_a': nrm(13, (N_EVEN, D, EVEN_IN), D ** -0.5),
        'g_q': 1.0 + nrm(14, (N_EVEN, Q_LORA), 0.02),
        'g_kv': 1.0 + nrm(15, (N_EVEN, KV_LORA), 0.02),
        'w_uq': nrm(16, (N_EVEN, Q_LORA, MLA_HEADS * (QK_NOPE + QK_ROPE)), Q_LORA ** -0.5),
        'w_ukv': nrm(17, (N_EVEN, KV_LORA, MLA_HEADS * (QK_NOPE + V_HEAD)), KV_LORA ** -0.5),
        'conv_w': nrm(18, (N_EVEN, CONV_W, CONV_DIM), CONV_W ** -0.5),
        'w_o_a': nrm(19, (N_EVEN, D, D), D ** -0.5),
        'w_in_c': nrm(20, (N_ODD, D, ML_IN), D ** -0.5),
        'b_gates': nrm(21, (N_ODD, 4 * ML_HEADS), 0.1) + gate_offset[None],
        'g_h': 1.0 + nrm(22, (N_ODD, ML_HEADS * ML_DV), 0.02),
        'w_o_c': nrm(23, (N_ODD, D, D), D ** -0.5),
        'w_ffn_gate': nrm(24, (N_EVEN, D, D_FF), D ** -0.5),
        'w_ffn_up': nrm(25, (N_EVEN, D, D_FF), D ** -0.5),
        'w_ffn_down': nrm(26, (N_EVEN, D_FF, D), D_FF ** -0.5),
        'w_router': nrm(27, (N_ODD, D, N_EXPERTS), D ** -0.5),
        'w_exp_gate': nrm(28, (N_ODD, N_EXPERTS, D, D_FF), D ** -0.5),
        'w_exp_up': nrm(29, (N_ODD, N_EXPERTS, D, D_FF), D ** -0.5),
        'w_exp_down': nrm(30, (N_ODD, N_EXPERTS, D_FF, D), D_FF ** -0.5),
        'g_final': 1.0 + nrm(31, (D,), 0.02),
    }


def reference(x_prompt, x_sample, c, cache_ckv, cache_krope, state_C, state_n, state_m,
              c_ctx, w_mod, b_mod, g_mix, g_ffn,
              w_in_a, g_q, g_kv, w_uq, w_ukv, conv_w, w_o_a,
              w_in_c, b_gates, g_h, w_o_c,
              w_ffn_gate, w_ffn_up, w_ffn_down,
              w_router, w_exp_gate, w_exp_up, w_exp_down, g_final):
    rope = axial_rope(x_sample.shape[1])
    cond_p = jax.nn.silu(c_ctx.astype(jnp.float32))[None]
    cond_s = jax.nn.silu(c.astype(jnp.float32))
    xp, xs = x_prompt, x_sample
    ckv_list, kr_list, C_list, n_list, m_list = [], [], [], [], []
    for l in range(DEPTH):
        j = l // 2
        mp = adaln(cond_p, w_mod[l], b_mod[l], xp.dtype)
        ms = adaln(cond_s, w_mod[l], b_mod[l], xs.dtype)
        hp = modulate(xp, g_mix[l], mp[0], mp[1])
        hs = modulate(xs, g_mix[l], ms[0], ms[1])
        if l % 2 == 0:
            op, ckv_p, kr_p = even_mixer(hp, None, None, None, w_in_a[j], g_q[j], g_kv[j], w_uq[j], w_ukv[j], conv_w[j], w_o_a[j])
            os_, _, _ = even_mixer(hs, rope, cache_ckv[:, j], cache_krope[:, j], w_in_a[j], g_q[j], g_kv[j], w_uq[j], w_ukv[j], conv_w[j], w_o_a[j])
            ckv_list.append(ckv_p)
            kr_list.append(kr_p)
        else:
            op, Cp, np_, mp_ = odd_mixer(hp, None, w_in_c[j], b_gates[j], g_h[j], w_o_c[j])
            os_, _, _, _ = odd_mixer(hs, (state_C[:, j], state_n[:, j], state_m[:, j]), w_in_c[j], b_gates[j], g_h[j], w_o_c[j])
            C_list.append(Cp)
            n_list.append(np_)
            m_list.append(mp_)
        xp = xp + mp[2] * op
        xs = xs + ms[2] * os_
        hp = modulate(xp, g_ffn[l], mp[3], mp[4])
        hs = modulate(xs, g_ffn[l], ms[3], ms[4])
        if l % 2 == 0:
            fp = swiglu(hp, w_ffn_gate[j], w_ffn_up[j], w_ffn_down[j])
            fs = swiglu(hs, w_ffn_gate[j], w_ffn_up[j], w_ffn_down[j])
        else:
            fp = moe_ffn(hp, w_router[j], w_exp_gate[j], w_exp_up[j], w_exp_down[j])
            fs = moe_ffn(hs, w_router[j], w_exp_gate[j], w_exp_up[j], w_exp_down[j])
        xp = xp + mp[5] * fp
        xs = xs + ms[5] * fs
    y_prompt = rmsnorm(xp, g_final)
    y_sample = rmsnorm(xs, g_final)
    new_cache_ckv = jnp.stack(ckv_list, axis=1)
    new_cache_krope = jnp.stack(kr_list, axis=1)
    new_state_C = jnp.stack(C_list, axis=1)
    new_state_n = jnp.stack(n_list, axis=1)
    new_state_m = jnp.stack(m_list, axis=1)
    return (y_prompt, y_sample, new_cache_ckv, new_cache_krope, new_state_C, new_state_n, new_state_m)
```

```python
import functools

import jax
import jax.numpy as jnp
from jax import lax
from jax.experimental import pallas as pl
from jax.experimental.pallas import tpu as pltpu

BF16 = jnp.bfloat16
F32 = jnp.float32

EPS = 1e-6
N_MOD = 6
GRID_W = 64
ROPE_BASE = 10000.0
MLA_HEADS = 8
Q_LORA = 512
KV_LORA = 512
QK_NOPE = 128
QK_ROPE = 64
V_HEAD = 128
CONV_W = 3
ML_HEADS = 4
TOP_K = 2

LANES = 128
VMEM_LIMIT = 56 * 1024 * 1024


def _params(*sem):
    return pltpu.CompilerParams(dimension_semantics=sem, vmem_limit_bytes=VMEM_LIMIT)


def _mm_kernel(*refs, n_a, n_g, n_x, n_o, prologue, epilogue):
    a_refs = refs[:n_a]
    w_refs = refs[n_a:n_a + n_a * n_g]
    x_refs = refs[n_a + n_a * n_g:n_a + n_a * n_g + n_x]
    o_refs = refs[n_a + n_a * n_g + n_x:n_a + n_a * n_g + n_x + n_o]
    wb_refs = refs[n_a + n_a * n_g + n_x + n_o:]

    @pl.when(pl.program_id(1) == 0)
    def _():
        for w, wb in zip(w_refs, wb_refs):
            wb[...] = w[...].astype(BF16)

    a_vals = []
    for a in a_refs:
        v = a[...]
        if prologue is not None:
            v = prologue(v)
        a_vals.append(v.astype(BF16))
    accs = []
    for g in range(n_g):
        acc = None
        for i in range(n_a):
            d = jnp.dot(a_vals[i], wb_refs[g * n_a + i][...], preferred_element_type=F32)
            acc = d if acc is None else acc + d
        accs.append(acc)
    outs = epilogue(accs, [x[...] for x in x_refs])
    if not isinstance(outs, (tuple, list)):
        outs = (outs,)
    for o, v in zip(o_refs, outs):
        o[...] = v.astype(o.dtype)


def matmul(a_list, w_groups, out_dtypes, *, n_cols, tm, tn, epilogue, extras=(), prologue=None,
           out_tn=None):
    n_a, n_g = len(a_list), len(w_groups)
    m = a_list[0][0].shape[0]
    out_tn = tn if out_tn is None else out_tn
    ks = []
    in_specs, args = [], []
    for arr, kb in a_list:
        k = arr.shape[1] if kb is None else kb[1]
        kidx = 0 if kb is None else kb[0]
        ks.append(k)
        in_specs.append(pl.BlockSpec((tm, k), lambda j, i, kidx=kidx: (i, kidx)))
        args.append(arr)
    scratch = []
    for grp in w_groups:
        for (arr, rb, cb), k in zip(grp, ks):
            in_specs.append(pl.BlockSpec((k, tn), lambda j, i, rb=rb, cb=cb: (rb, j + cb)))
            args.append(arr)
            scratch.append(pltpu.VMEM((k, tn), BF16))
    for arr, bs, im in extras:
        in_specs.append(pl.BlockSpec(bs, im))
        args.append(arr)
    n_tiles = n_cols // tn
    out_shape = [jax.ShapeDtypeStruct((m, n_tiles * out_tn), dt) for dt in out_dtypes]
    out_specs = [pl.BlockSpec((tm, out_tn), lambda j, i: (i, j)) for _ in out_dtypes]
    kern = functools.partial(_mm_kernel, n_a=n_a, n_g=n_g, n_x=len(extras), n_o=len(out_dtypes),
                             prologue=prologue, epilogue=epilogue)
    outs = pl.pallas_call(
        kern, grid=(n_tiles, m // tm), in_specs=in_specs, out_specs=out_specs, out_shape=out_shape,
        scratch_shapes=scratch, compiler_params=_params("arbitrary", "arbitrary"))(*args)
    return outs


def _ep_plain(accs, xs):
    return accs[0]


def _ep_bias(accs, xs):
    return accs[0] + xs[0]


def _ep_residual(accs, xs):
    return xs[0] + xs[1] * accs[0]


def _ep_swiglu(accs, xs):
    g = accs[0]
    return (g * jax.nn.sigmoid(g)) * accs[1]


def _ep_rmsnorm(accs, xs):
    acc = accs[0]
    y = acc * lax.rsqrt(jnp.mean(acc * acc, axis=-1, keepdims=True) + EPS) * xs[0]
    return y, y


def _silu(x):
    return x * jax.nn.sigmoid(x)


def _modulate_kernel(x_ref, g_ref, sh_ref, sc_ref, o_ref):
    x = x_ref[...]
    y = x * lax.rsqrt(jnp.mean(x * x, axis=-1, keepdims=True) + EPS) * g_ref[...]
    o_ref[...] = (y * (1.0 + sc_ref[...]) + sh_ref[...]).astype(o_ref.dtype)


def _mod_spec(tok, chunk, width, col_of):
    def im(*g):
        i, j = col_of(*g)
        return (tok.mod_row(i) * N_MOD + chunk, 0, j)
    return ((None, 1, width), im)


class _Tokens:
    def __init__(self, n_p, t_p, n_s, t_s):
        self.n_p, self.t_p, self.n_s, self.t_s = n_p, t_p, n_s, t_s
        self.m = n_p + n_s

    def with_tile(self, tm):
        assert self.n_p % tm == 0 and self.t_s % tm == 0
        t = _Tokens(self.n_p, self.t_p, self.n_s, self.t_s)
        t.tm = tm
        return t

    def mod_row(self, i):
        start = i * self.tm
        return jnp.where(start < self.n_p, 0, 1 + (start - self.n_p) // self.t_s)


def modulate(x, g, mod, tok, shift_chunk, scale_chunk, tr=512):
    m, d = x.shape
    tk = tok.with_tile(tr)
    sh_bs, sh_im = _mod_spec(tk, shift_chunk, d, lambda i: (i, 0))
    sc_bs, sc_im = _mod_spec(tk, scale_chunk, d, lambda i: (i, 0))
    return pl.pallas_call(
        _modulate_kernel, grid=(m // tr,),
        in_specs=[pl.BlockSpec((tr, d), lambda i: (i, 0)), pl.BlockSpec((1, d), lambda i: (0, 0)),
                  pl.BlockSpec(sh_bs, sh_im), pl.BlockSpec(sc_bs, sc_im)],
        out_specs=pl.BlockSpec((tr, d), lambda i: (i, 0)),
        out_shape=jax.ShapeDtypeStruct((m, d), BF16),
        compiler_params=_params("arbitrary"))(x, g, mod, mod)


def _router_modulate_kernel(x_ref, g_ref, sh_ref, sc_ref, wr_ref, o_ref, gate_ref, *, n_experts):
    x = x_ref[...]
    y = x * lax.rsqrt(jnp.mean(x * x, axis=-1, keepdims=True) + EPS) * g_ref[...]
    h = y * (1.0 + sc_ref[...]) + sh_ref[...]
    o_ref[...] = h.astype(o_ref.dtype)
    w = wr_ref[...]
    h1 = h.astype(BF16)
    h2 = (h - h1.astype(F32)).astype(BF16)
    w1 = w.astype(BF16)
    w2 = (w - w1.astype(F32)).astype(BF16)
    logits = (jnp.dot(h1, w1, preferred_element_type=F32) + jnp.dot(h1, w2, preferred_element_type=F32)
              + jnp.dot(h2, w1, preferred_element_type=F32))
    lane = lax.broadcasted_iota(jnp.int32, logits.shape, 1).astype(F32)
    neg = jnp.float32(-jnp.inf)
    lg = jnp.where(lane < n_experts, logits, neg)
    m1 = jnp.max(lg, axis=-1, keepdims=True)
    i1 = jnp.min(jnp.where(lg == m1, lane, float(LANES)), axis=-1, keepdims=True)
    lg2 = jnp.where(lane == i1, neg, lg)
    m2 = jnp.max(lg2, axis=-1, keepdims=True)
    i2 = jnp.min(jnp.where(lg2 == m2, lane, float(LANES)), axis=-1, keepdims=True)
    e2 = jnp.exp(m2 - m1)
    p1 = 1.0 / (1.0 + e2)
    p2 = e2 / (1.0 + e2)
    gate_ref[...] = jnp.where(lane == i1, p1, 0.0) + jnp.where(lane == i2, p2, 0.0)


def router_modulate(x, g, mod, tok, shift_chunk, scale_chunk, w_router_padded, n_experts, tr=512):
    m, d = x.shape
    tk = tok.with_tile(tr)
    sh_bs, sh_im = _mod_spec(tk, shift_chunk, d, lambda i: (i, 0))
    sc_bs, sc_im = _mod_spec(tk, scale_chunk, d, lambda i: (i, 0))
    return pl.pallas_call(
        functools.partial(_router_modulate_kernel, n_experts=n_experts), grid=(m // tr,),
        in_specs=[pl.BlockSpec((tr, d), lambda i: (i, 0)), pl.BlockSpec((1, d), lambda i: (0, 0)),
                  pl.BlockSpec(sh_bs, sh_im), pl.BlockSpec(sc_bs, sc_im),
                  pl.BlockSpec((d, LANES), lambda i: (0, 0))],
        out_specs=[pl.BlockSpec((tr, d), lambda i: (i, 0)), pl.BlockSpec((tr, LANES), lambda i: (i, 0))],
        out_shape=[jax.ShapeDtypeStruct((m, d), BF16), jax.ShapeDtypeStruct((m, LANES), F32)],
        compiler_params=_params("arbitrary"))(x, g, mod, mod, w_router_padded)


def _rmsnorm_kernel(x_ref, g_ref, o_ref):
    x = x_ref[...]
    o_ref[...] = x * lax.rsqrt(jnp.mean(x * x, axis=-1, keepdims=True) + EPS) * g_ref[...]


def rmsnorm_rows(x, g, tr=512):
    m, d = x.shape
    return pl.pallas_call(
        _rmsnorm_kernel, grid=(m // tr,),
        in_specs=[pl.BlockSpec((tr, d), lambda i: (i, 0)), pl.BlockSpec((1, d), lambda i: (0, 0))],
        out_specs=pl.BlockSpec((tr, d), lambda i: (i, 0)),
        out_shape=jax.ShapeDtypeStruct((m, d), F32),
        compiler_params=_params("arbitrary"))(x, g)


def _make_conv_epilogue(tok, tm):
    def ep(accs, xs):
        ux, ub, uc = accs
        cw = xs[0]
        p = uc * ux
        i = pl.program_id(1)
        r = lax.broadcasted_iota(jnp.int32, p.shape, 0)
        is_p = i * tm < tok.n_p
        t = jnp.where(is_p, r % tok.t_p, r % tok.t_s)
        t_len = jnp.where(is_p, tok.t_p, tok.t_s)
        prev = jnp.where(t == 0, 0.0, pltpu.roll(p, 1, axis=0))
        nxt = jnp.where(t == t_len - 1, 0.0, pltpu.roll(p, tm - 1, axis=0))
        return ub * (prev * cw[0:1, :] + p * cw[1:2, :] + nxt * cw[2:3, :])
    return ep


def _swap16(x):
    n = x.shape[-1]
    lane = lax.broadcasted_iota(jnp.int32, x.shape, x.ndim - 1)
    fwd = pltpu.roll(x, n - 16, axis=x.ndim - 1)
    bwd = pltpu.roll(x, 16, axis=x.ndim - 1)
    return jnp.where((lane // 16) % 2 == 0, fwd, bwd)


def _attn_kernel(*refs, n_heads, with_ctx, scale):
    if with_ctx:
        (qn_ref, qr_ref, kv_ref, kr_ref, ckv_ref, ckr_ref, cq_ref, sq_ref, ck_ref, sk_ref, o_ref) = refs
    else:
        qn_ref, qr_ref, kv_ref, kr_ref, o_ref = refs
    qr = qr_ref[...]
    kr = kr_ref[...][:, :QK_ROPE]
    if with_ctx:
        qr = qr * cq_ref[...] + _swap16(qr) * sq_ref[...]
        kr128 = kr_ref[...]
        kr = (kr128 * ck_ref[...] + _swap16(kr128) * sk_ref[...])[:, :QK_ROPE]
    qr = qr.astype(BF16)
    kr = kr.astype(BF16)
    if with_ctx:
        ckr = ckr_ref[...].astype(BF16)
    dn = (((1,), (1,)), ((), ()))
    for h in range(n_heads):
        q = jnp.concatenate([qn_ref[:, h * QK_NOPE:(h + 1) * QK_NOPE],
                             qr[:, h * QK_ROPE:(h + 1) * QK_ROPE]], axis=1)
        c0 = h * (QK_NOPE + V_HEAD)
        k = jnp.concatenate([kv_ref[:, c0:c0 + QK_NOPE], kr], axis=1)
        s = lax.dot_general(q, k, dn, preferred_element_type=F32) * scale
        mx = jnp.max(s, axis=-1, keepdims=True)
        if with_ctx:
            kc = jnp.concatenate([ckv_ref[:, c0:c0 + QK_NOPE], ckr], axis=1)
            s2 = lax.dot_general(q, kc, dn, preferred_element_type=F32) * scale
            mx = jnp.maximum(mx, jnp.max(s2, axis=-1, keepdims=True))
        p = jnp.exp(s - mx)
        den = jnp.sum(p, axis=-1, keepdims=True)
        o = jnp.dot(p.astype(BF16), kv_ref[:, c0 + QK_NOPE:c0 + QK_NOPE + V_HEAD],
                    preferred_element_type=F32)
        if with_ctx:
            p2 = jnp.exp(s2 - mx)
            den = den + jnp.sum(p2, axis=-1, keepdims=True)
            o = o + jnp.dot(p2.astype(BF16), ckv_ref[:, c0 + QK_NOPE:c0 + QK_NOPE + V_HEAD],
                            preferred_element_type=F32)
        o_ref[:, h * V_HEAD:(h + 1) * V_HEAD] = (o / den).astype(o_ref.dtype)


def attention(qn, qr, kv, kr, *, row0, n_batch, t, tq, ctx=None):
    h = MLA_HEADS
    nq = t // tq
    qb0 = row0 // tq
    kb0 = row0 // t
    scale = float(QK_NOPE + QK_ROPE) ** -0.5
    in_specs = [pl.BlockSpec((tq, h * QK_NOPE), lambda b, i: (qb0 + b * nq + i, 0)),
                pl.BlockSpec((tq, h * QK_ROPE), lambda b, i: (qb0 + b * nq + i, 0)),
                pl.BlockSpec((t, h * (QK_NOPE + V_HEAD)), lambda b, i: (kb0 + b, 0)),
                pl.BlockSpec((t, LANES), lambda b, i: (kb0 + b, 0))]
    args = [qn, qr, kv, kr]
    if ctx is not None:
        ckv, ckr, past, (cq, sq, ck, sk) = ctx
        in_specs += [pl.BlockSpec((past, h * (QK_NOPE + V_HEAD)), lambda b, i: (b, 0)),
                     pl.BlockSpec((past, QK_ROPE), lambda b, i: (b, 0)),
                     pl.BlockSpec((tq, h * QK_ROPE), lambda b, i: (i, 0)),
                     pl.BlockSpec((tq, h * QK_ROPE), lambda b, i: (i, 0)),
                     pl.BlockSpec((t, LANES), lambda b, i: (0, 0)),
                     pl.BlockSpec((t, LANES), lambda b, i: (0, 0))]
        args += [ckv, ckr, cq, sq, ck, sk]
    return pl.pallas_call(
        functools.partial(_attn_kernel, n_heads=h, with_ctx=ctx is not None, scale=scale),
        grid=(n_batch, nq), in_specs=in_specs,
        out_specs=pl.BlockSpec((tq, h * V_HEAD), lambda b, i: (b * nq + i, 0)),
        out_shape=jax.ShapeDtypeStruct((n_batch * t, h * V_HEAD), BF16),
        compiler_params=_params("arbitrary", "arbitrary"))(*args)


def _rope_tables(t):
    rows = t // GRID_W
    row = jnp.repeat(jnp.arange(rows, dtype=F32), GRID_W)
    col = jnp.tile(jnp.arange(GRID_W, dtype=F32), rows)
    half = QK_ROPE // 2
    inv_freq = ROPE_BASE ** (-jnp.arange(0, half, 2, dtype=F32) / half)
    ang_r = row[:, None] * inv_freq
    ang_c = col[:, None] * inv_freq
    cr, sr, cc, sc = jnp.cos(ang_r), jnp.sin(ang_r), jnp.cos(ang_c), jnp.sin(ang_c)
    cos = jnp.concatenate([cr, cr, cc, cc], axis=-1)
    sin = jnp.concatenate([-sr, sr, -sc, sc], axis=-1)
    return cos, sin


def _log_sigmoid(x):
    return jnp.minimum(x, 0.0) - jnp.log(1.0 + jnp.exp(-jnp.abs(x)))


def _mlstm_kernel(*refs, chunk, n_heads, dk, dv, has_init):
    if has_init:
        (q_ref, k_ref, v_ref, g_ref, gt_ref, bg_ref, bgt_ref, c0_ref, n0_ref, m0_ref,
         h_ref, c_ref, n_ref, m_ref) = refs
    else:
        q_ref, k_ref, v_ref, g_ref, gt_ref, bg_ref, bgt_ref, h_ref, c_ref, n_ref, m_ref = refs
    d = pl.program_id(1)

    @pl.when(pl.program_id(2) == 0)
    def _():
        if has_init:
            c_ref[...] = c0_ref[...]
            n_ref[...] = n0_ref[...]
            m_ref[...] = m0_ref[...]
        else:
            c_ref[...] = jnp.zeros_like(c_ref)
            n_ref[...] = jnp.zeros_like(n_ref)
            m_ref[...] = jnp.zeros_like(m_ref)

    L = chunk
    row = lax.broadcasted_iota(jnp.int32, (L, L), 0)
    col = lax.broadcasted_iota(jnp.int32, (L, L), 1)
    sgn = jnp.where(d == 0, 1, -1)
    mask = (col - row) * sgn <= 0
    mask_t = (row - col) * sgn <= 0
    neg = jnp.float32(-jnp.inf)
    g = g_ref[...] + bg_ref[...]
    gt = gt_ref[...] + bgt_ref[...]
    dn_t = (((0,), (0,)), ((), ()))
    for h in range(n_heads):
        i_col = g[:, h:h + 1]
        f_col = _log_sigmoid(g[:, n_heads + h:n_heads + h + 1])
        i_row = gt[h:h + 1, :]
        f_row = _log_sigmoid(gt[n_heads + h:n_heads + h + 1, :])
        b_col = jnp.sum(jnp.where(mask, f_row, 0.0), axis=1, keepdims=True)
        b_row = jnp.sum(jnp.where(mask_t, f_col, 0.0), axis=0, keepdims=True)
        m_old = m_ref[:, h:h + 1]
        dmat = jnp.where(mask, b_col - b_row + i_row, neg)
        a_col = b_col + m_old
        mrow = jnp.maximum(a_col, jnp.max(dmat, axis=1, keepdims=True))
        w_intra = jnp.exp(dmat - mrow)
        w_inter = jnp.exp(a_col - mrow)
        qh = q_ref[:, h * dk:(h + 1) * dk]
        kh = k_ref[:, h * dk:(h + 1) * dk] * (dk ** -0.5)
        vh = v_ref[:, h * dv:(h + 1) * dv]
        c_old = c_ref[h]
        n_old = n_ref[h:h + 1, :]
        s = lax.dot_general(qh, kh, (((1,), (1,)), ((), ())), preferred_element_type=F32) * w_intra
        num = (w_inter * jnp.dot(qh, c_old.astype(BF16), preferred_element_type=F32)
               + jnp.dot(s.astype(BF16), vh, preferred_element_type=F32))
        qn = jnp.sum(qh.astype(F32) * n_old, axis=1, keepdims=True)
        den = w_inter * qn + jnp.sum(s, axis=1, keepdims=True)
        h_ref[:, h * dv:(h + 1) * dv] = num / jnp.maximum(jnp.abs(den), jnp.exp(-mrow))
        bl = jnp.sum(f_row, axis=1, keepdims=True)
        g_row = bl - b_row + i_row
        g_col = bl - b_col + i_col
        m_new = jnp.maximum(bl + m_old, jnp.max(g_row, axis=1, keepdims=True))
        wg_col = jnp.exp(g_col - m_new)
        decay = jnp.exp(bl + m_old - m_new)
        kw = kh.astype(F32) * wg_col
        c_ref[h] = decay * c_old + lax.dot_general(kw.astype(BF16), vh, dn_t, preferred_element_type=F32)
        n_ref[h:h + 1, :] = decay * n_old + jnp.sum(kw, axis=0, keepdims=True)
        m_ref[:, h:h + 1] = m_new


def mlstm(qkv, gates, gates_t, bg, bg_t, *, row0, n_batch, t, chunk, init=None):
    nh = ML_HEADS
    hv = qkv.shape[1] // 2
    dv = hv // nh
    dk = dv // 2
    hk = nh * dk
    nc = t // chunk
    rb0 = row0 // chunk

    def blk(b, d, c):
        return rb0 + b * nc + jnp.where(d == 0, c, nc - 1 - c)

    in_specs = [pl.BlockSpec((chunk, hk), lambda b, d, c: (blk(b, d, c), 0)),
                pl.BlockSpec((chunk, hk), lambda b, d, c: (blk(b, d, c), 1)),
                pl.BlockSpec((chunk, hv), lambda b, d, c: (blk(b, d, c), 1)),
                pl.BlockSpec((None, chunk, 2 * nh), lambda b, d, c: (d, blk(b, d, c), 0)),
                pl.BlockSpec((None, 2 * nh, chunk), lambda b, d, c: (d, 0, blk(b, d, c))),
                pl.BlockSpec((None, 1, 2 * nh), lambda b, d, c: (d, 0, 0)),
                pl.BlockSpec((None, 2 * nh, 1), lambda b, d, c: (d, 0, 0))]
    args = [qkv, qkv, qkv, gates, gates_t, bg, bg_t]
    if init is not None:
        in_specs += [pl.BlockSpec((None, None, nh, dk, dv), lambda b, d, c: (b, d, 0, 0, 0)),
                     pl.BlockSpec((None, None, nh, dk), lambda b, d, c: (b, d, 0, 0)),
                     pl.BlockSpec((None, None, 1, nh), lambda b, d, c: (b, d, 0, 0))]
        args += list(init)
    out_specs = [pl.BlockSpec((None, chunk, hv), lambda b, d, c: (d, blk(b, d, c) - rb0, 0)),
                 pl.BlockSpec((None, None, nh, dk, dv), lambda b, d, c: (b, d, 0, 0, 0)),
                 pl.BlockSpec((None, None, nh, dk), lambda b, d, c: (b, d, 0, 0)),
                 pl.BlockSpec((None, None, 1, nh), lambda b, d, c: (b, d, 0, 0))]
    out_shape = [jax.ShapeDtypeStruct((2, n_batch * t, hv), F32),
                 jax.ShapeDtypeStruct((n_batch, 2, nh, dk, dv), F32),
                 jax.ShapeDtypeStruct((n_batch, 2, nh, dk), F32),
                 jax.ShapeDtypeStruct((n_batch, 2, 1, nh), F32)]
    return pl.pallas_call(
        functools.partial(_mlstm_kernel, chunk=chunk, n_heads=nh, dk=dk, dv=dv, has_init=init is not None),
        grid=(n_batch, 2, nc), in_specs=in_specs, out_specs=out_specs, out_shape=out_shape,
        compiler_params=_params("arbitrary", "arbitrary", "arbitrary"))(*args)


def _mlstm_out_kernel(hf_ref, hb_ref, o_ref, g_ref, y_ref, *, n_heads, dv):
    for h in range(n_heads):
        sl = slice(h * dv, (h + 1) * dv)
        x = hf_ref[:, sl] + hb_ref[:, sl]
        ht = x * lax.rsqrt(jnp.mean(x * x, axis=-1, keepdims=True) + EPS) * g_ref[:, sl]
        y_ref[:, sl] = (jax.nn.sigmoid(o_ref[:, sl]) * ht).astype(y_ref.dtype)


def mlstm_out(h2, o, g_h, tr=256):
    _, m, hv = h2.shape
    return pl.pallas_call(
        functools.partial(_mlstm_out_kernel, n_heads=ML_HEADS, dv=hv // ML_HEADS), grid=(m // tr,),
        in_specs=[pl.BlockSpec((None, tr, hv), lambda i: (0, i, 0)),
                  pl.BlockSpec((None, tr, hv), lambda i: (1, i, 0)),
                  pl.BlockSpec((tr, hv), lambda i: (i, 0)),
                  pl.BlockSpec((1, hv), lambda i: (0, 0))],
        out_specs=pl.BlockSpec((tr, hv), lambda i: (i, 0)),
        out_shape=jax.ShapeDtypeStruct((m, hv), BF16),
        compiler_params=_params("arbitrary"))(h2, h2, o, g_h)


def _adaln(cond, w, b):
    rows, d = cond.shape
    n = w.shape[1]
    tn = n // 8
    out, = matmul([(cond, None)], [[(w, 0, 0)]], [F32], n_cols=n, tm=rows, tn=tn, epilogue=_ep_bias,
                  extras=[(b.reshape(1, n), (1, tn), lambda j, i: (0, j))], prologue=_silu)
    return out.reshape(rows * N_MOD, 1, d)


def _residual_extras(x, mod, tok, tm, tn, chunk):
    tk = tok.with_tile(tm)
    bs, im = _mod_spec(tk, chunk, tn, lambda j, i: (i, j))
    return [(x, (tm, tn), lambda j, i: (i, j)), (mod, bs, im)]


def kernel(x_prompt, x_sample, c, cache_ckv, cache_krope, state_C, state_n, state_m, c_ctx, w_mod, b_mod, g_mix, g_ffn, w_in_a, g_q, g_kv, w_uq, w_ukv, conv_w, w_o_a, w_in_c, b_gates, g_h, w_o_c, w_ffn_gate, w_ffn_up, w_ffn_down, w_router, w_exp_gate, w_exp_up, w_exp_down, g_final):
    bp, tp, d = x_prompt.shape
    bs, ts, _ = x_sample.shape
    past = cache_ckv.shape[2]
    n_p, n_s = bp * tp, bs * ts
    tok = _Tokens(n_p, tp, n_s, ts)
    m = tok.m
    assert n_p % ts == 0 and ts % 512 == 0 and tp % 256 == 0
    x = jnp.concatenate([x_prompt.reshape(n_p, d), x_sample.reshape(n_s, d)], axis=0)

    n_cond = -(-(1 + bs) // 8) * 8
    cond = jnp.concatenate([c_ctx[None], c, jnp.zeros((n_cond - 1 - bs, d), F32)], axis=0)

    TM, TN = 512, 512

    mod = _adaln(cond, w_mod[0], b_mod[0])
    h = modulate(x, g_mix[0].reshape(1, d), mod, tok, 0, 1)
    w_in = w_in_a[0]
    conv_dim = conv_w.shape[2]
    c_kr = Q_LORA + KV_LORA
    c_ux = c_kr + QK_ROPE
    gq_kv = jnp.concatenate([g_q[0], g_kv[0]])[None]
    lat_f32, lat_bf = matmul([(h, None)], [[(w_in, 0, 0)]], [F32, BF16], n_cols=c_kr, tm=TM, tn=Q_LORA,
                             epilogue=_ep_rmsnorm, extras=[(gq_kv, (1, Q_LORA), lambda j, i: (0, j))])
    w_kr = jnp.pad(w_in[:, c_kr:c_ux], ((0, 0), (0, LANES - QK_ROPE)))
    kr, = matmul([(h, None)], [[(w_kr, 0, 0)]], [F32], n_cols=LANES, tm=TM, tn=LANES, epilogue=_ep_plain)
    w_conv = w_in[:, c_ux:]
    cb = 256
    ncb = conv_dim // cb
    conv, = matmul([(h, None)], [[(w_conv, 0, g * ncb)] for g in range(3)], [BF16], n_cols=conv_dim,
                   tm=ts, tn=cb, epilogue=_make_conv_epilogue(tok, ts),
                   extras=[(conv_w[0], (CONV_W, cb), lambda j, i: (0, j))])
    wq = w_uq[0].reshape(Q_LORA, MLA_HEADS, QK_NOPE + QK_ROPE)
    wq_n = wq[:, :, :QK_NOPE].reshape(Q_LORA, MLA_HEADS * QK_NOPE)
    wq_r = wq[:, :, QK_NOPE:].reshape(Q_LORA, MLA_HEADS * QK_ROPE)
    qn, = matmul([(lat_bf, (0, Q_LORA))], [[(wq_n, 0, 0)]], [BF16], n_cols=MLA_HEADS * QK_NOPE, tm=TM, tn=TN,
                 epilogue=_ep_plain)
    qr, = matmul([(lat_bf, (0, Q_LORA))], [[(wq_r, 0, 0)]], [F32], n_cols=MLA_HEADS * QK_ROPE, tm=TM, tn=TN,
                 epilogue=_ep_plain)
    n_kv = MLA_HEADS * (QK_NOPE + V_HEAD)
    kv, = matmul([(lat_bf, (1, KV_LORA))], [[(w_ukv[0], 0, 0)]], [BF16], n_cols=n_kv, tm=TM, tn=TN,
                 epilogue=_ep_plain)
    ctx_kv, = matmul([(cache_ckv[:, 0].reshape(bs * past, KV_LORA), None)], [[(w_ukv[0], 0, 0)]], [BF16],
                     n_cols=n_kv, tm=TM, tn=TN, epilogue=_ep_plain)
    ctx_kr = cache_krope[:, 0].reshape(bs * past, QK_ROPE)
    cos, sin = _rope_tables(ts)
    pad = ((0, 0), (0, LANES - QK_ROPE))
    tables = (jnp.tile(cos, (1, MLA_HEADS)), jnp.tile(sin, (1, MLA_HEADS)), jnp.pad(cos, pad), jnp.pad(sin, pad))
    att_p = attention(qn, qr, kv, kr, row0=0, n_batch=bp, t=tp, tq=tp)
    att_s = attention(qn, qr, kv, kr, row0=n_p, n_batch=bs, t=ts, tq=256, ctx=(ctx_kv, ctx_kr, past, tables))
    att = jnp.concatenate([att_p, att_s], axis=0)
    half = MLA_HEADS * V_HEAD
    assert half == conv_dim
    x, = matmul([(att, None), (conv, None)], [[(w_o_a[0], 0, 0), (w_o_a[0], 1, 0)]], [F32], n_cols=d, tm=TM,
                tn=TN, epilogue=_ep_residual, extras=_residual_extras(x, mod, tok, TM, TN, 2))
    h = modulate(x, g_ffn[0].reshape(1, d), mod, tok, 3, 4)
    d_ff = w_ffn_gate.shape[2]
    hid, = matmul([(h, None)], [[(w_ffn_gate[0], 0, 0)], [(w_ffn_up[0], 0, 0)]], [BF16], n_cols=d_ff, tm=TM,
                  tn=TN, epilogue=_ep_swiglu)
    x, = matmul([(hid, None)], [[(w_ffn_down[0], 0, 0)]], [F32], n_cols=d, tm=TM, tn=256,
                epilogue=_ep_residual, extras=_residual_extras(x, mod, tok, TM, 256, 5))
    ckv_p = lat_f32[:n_p, Q_LORA:].reshape(bp, 1, tp, KV_LORA)
    kr_p = kr[:n_p, :QK_ROPE].reshape(bp, 1, tp, QK_ROPE)

    mod = _adaln(cond, w_mod[1], b_mod[1])
    h = modulate(x, g_mix[1].reshape(1, d), mod, tok, 0, 1)
    w_c = w_in_c[0]
    hv = d
    hk = hv // 2
    n_qkv = 2 * hk + hv
    qkv, = matmul([(h, None)], [[(w_c, 0, 0)]], [BF16], n_cols=n_qkv, tm=TM, tn=TN, epilogue=_ep_plain)
    o_gate, = matmul([(h, None)], [[(w_c, 0, n_qkv // TN)]], [F32], n_cols=hv, tm=TM, tn=TN, epilogue=_ep_plain)
    n_g = 4 * ML_HEADS
    w_g = jnp.pad(w_c[:, n_qkv + hv:], ((0, 0), (0, LANES - n_g)))
    graw, = matmul([(h, None)], [[(w_g, 0, 0)]], [F32], n_cols=LANES, tm=TM, tn=LANES, epilogue=_ep_plain)
    gates = graw[:, :n_g].reshape(m, 2, 2 * ML_HEADS).transpose(1, 0, 2)
    gates_t = gates.transpose(0, 2, 1)
    bg = b_gates[0].reshape(2, 1, 2 * ML_HEADS)
    bg_t = bg.transpose(0, 2, 1)
    chunk = 128
    h_p, c_p, n_p_state, m_p = mlstm(qkv, gates, gates_t, bg, bg_t, row0=0, n_batch=bp, t=tp, chunk=chunk)
    init = (state_C[:, 0], state_n[:, 0], state_m[:, 0].reshape(bs, 2, 1, ML_HEADS))
    h_s, _, _, _ = mlstm(qkv, gates, gates_t, bg, bg_t, row0=n_p, n_batch=bs, t=ts, chunk=chunk, init=init)
    y = mlstm_out(jnp.concatenate([h_p, h_s], axis=1), o_gate, g_h[0].reshape(1, hv))
    x, = matmul([(y, None)], [[(w_o_c[0], 0, 0)]], [F32], n_cols=d, tm=TM, tn=TN, epilogue=_ep_residual,
                extras=_residual_extras(x, mod, tok, TM, TN, 2))
    n_exp = w_router.shape[2]
    w_r = jnp.pad(w_router[0], ((0, 0), (0, LANES - n_exp)))
    h, dense_gate = router_modulate(x, g_ffn[1].reshape(1, d), mod, tok, 3, 4, w_r, n_exp)
    tk = tok.with_tile(TM)
    mbs, mim = _mod_spec(tk, 5, 256, lambda j, i: (i, j))
    for e in range(n_exp):
        hid, = matmul([(h, None)], [[(w_exp_gate[0, e], 0, 0)], [(w_exp_up[0, e], 0, 0)]], [BF16], n_cols=d_ff,
                      tm=TM, tn=TN, epilogue=_ep_swiglu)

        def ep(accs, xs, e=e):
            return xs[0] + (xs[1] * xs[2][:, e:e + 1]) * accs[0]

        x, = matmul([(hid, None)], [[(w_exp_down[0, e], 0, 0)]], [F32], n_cols=d, tm=TM, tn=256, epilogue=ep,
                    extras=[(x, (TM, 256), lambda j, i: (i, j)), (mod, mbs, mim),
                            (dense_gate, (TM, LANES), lambda j, i: (i, 0))])
    y = rmsnorm_rows(x, g_final.reshape(1, d))
    y_prompt = y[:n_p].reshape(bp, tp, d)
    y_sample = y[n_p:].reshape(bs, ts, d)
    new_c = c_p[:, None]
    new_n = n_p_state[:, None]
    new_m = m_p.reshape(bp, 1, 2, ML_HEADS)
    return (y_prompt, y_sample, ckv_p, kr_p, new_c, new_n, new_m)
```

```python
import functools

import jax
import jax.numpy as jnp
from jax import lax
from jax.experimental import pallas as pl
from jax.experimental.pallas import tpu as pltpu

BF16 = jnp.bfloat16
F32 = jnp.float32

EPS = 1e-6
N_MOD = 6
GRID_W = 64
ROPE_BASE = 10000.0
MLA_HEADS = 8
Q_LORA = 512
KV_LORA = 512
QK_NOPE = 128
QK_ROPE = 64
V_HEAD = 128
CONV_W = 3
ML_HEADS = 4
TOP_K = 2

LANES = 128
VMEM_LIMIT = 56 * 1024 * 1024


def _params(*sem):
    return pltpu.CompilerParams(dimension_semantics=sem, vmem_limit_bytes=VMEM_LIMIT)


def _pick_rows(refs, n_prompt_tiles, i):
    if len(refs) == 1:
        return refs[0][...]
    return jnp.where(i < n_prompt_tiles, refs[0][...], refs[1][...])


def _pair_specs(arr, block, n_prompt_tiles, col_of):
    if not isinstance(arr, tuple):
        return [pl.BlockSpec(block, lambda *g: col_of(*g))], [arr]
    npt = n_prompt_tiles

    def im_p(*g):
        i, c = col_of(*g)
        return (jnp.minimum(i, npt - 1), c)

    def im_s(*g):
        i, c = col_of(*g)
        return (jnp.maximum(i - npt, 0), c)

    return [pl.BlockSpec(block, im_p), pl.BlockSpec(block, im_s)], list(arr)


def _mm_kernel(*refs, a_counts, n_g, n_x, n_o, prologue, epilogue, n_prompt_tiles):
    n_a = len(a_counts)
    n_ar = sum(a_counts)
    a_refs = refs[:n_ar]
    w_refs = refs[n_ar:n_ar + n_a * n_g]
    x_refs = refs[n_ar + n_a * n_g:n_ar + n_a * n_g + n_x]
    o_refs = refs[n_ar + n_a * n_g + n_x:n_ar + n_a * n_g + n_x + n_o]
    wb_refs = refs[n_ar + n_a * n_g + n_x + n_o:]

    @pl.when(pl.program_id(1) == 0)
    def _():
        for w, wb in zip(w_refs, wb_refs):
            wb[...] = w[...].astype(BF16)

    a_vals = []
    pos = 0
    for cnt in a_counts:
        v = _pick_rows(a_refs[pos:pos + cnt], n_prompt_tiles, pl.program_id(1))
        pos += cnt
        if prologue is not None:
            v = prologue(v)
        a_vals.append(v.astype(BF16))
    accs = []
    for g in range(n_g):
        acc = None
        for i in range(n_a):
            d = jnp.dot(a_vals[i], wb_refs[g * n_a + i][...], preferred_element_type=F32)
            acc = d if acc is None else acc + d
        accs.append(acc)
    outs = epilogue(accs, [x[...] for x in x_refs])
    if not isinstance(outs, (tuple, list)):
        outs = (outs,)
    for o, v in zip(o_refs, outs):
        o[...] = v.astype(o.dtype)


def matmul(a_list, w_groups, out_dtypes, *, n_cols, tm, tn, epilogue, extras=(), prologue=None,
           out_tn=None, n_prompt_rows=0, w_buffers=2):
    n_g = len(w_groups)
    npt = n_prompt_rows // tm
    out_tn = tn if out_tn is None else out_tn
    ks, a_counts = [], []
    in_specs, args = [], []
    for arr, kb in a_list:
        first = arr[0] if isinstance(arr, tuple) else arr
        k = first.shape[1] if kb is None else kb[1]
        kidx = 0 if kb is None else kb[0]
        ks.append(k)
        specs, arrs = _pair_specs(arr, (tm, k), npt, lambda j, i, kidx=kidx: (i, kidx))
        in_specs += specs
        args += arrs
        a_counts.append(len(arrs))
    arr0 = a_list[0][0]
    m = sum(a.shape[0] for a in arr0) if isinstance(arr0, tuple) else arr0.shape[0]
    w_kwargs = {} if w_buffers == 2 else {"pipeline_mode": pl.Buffered(w_buffers)}
    scratch = []
    for grp in w_groups:
        for (arr, rb, cb), k in zip(grp, ks):
            in_specs.append(pl.BlockSpec((k, tn), lambda j, i, rb=rb, cb=cb: (rb, j + cb), **w_kwargs))
            args.append(arr)
            scratch.append(pltpu.VMEM((k, tn), BF16))
    for arr, bs, im in extras:
        in_specs.append(pl.BlockSpec(bs, im))
        args.append(arr)
    n_tiles = n_cols // tn
    out_shape = [jax.ShapeDtypeStruct((m, n_tiles * out_tn), dt) for dt in out_dtypes]
    out_specs = [pl.BlockSpec((tm, out_tn), lambda j, i: (i, j)) for _ in out_dtypes]
    kern = functools.partial(_mm_kernel, a_counts=tuple(a_counts), n_g=n_g, n_x=len(extras), n_o=len(out_dtypes),
                             prologue=prologue, epilogue=epilogue, n_prompt_tiles=npt)
    outs = pl.pallas_call(
        kern, grid=(n_tiles, m // tm), in_specs=in_specs, out_specs=out_specs, out_shape=out_shape,
        scratch_shapes=scratch, compiler_params=_params("arbitrary", "arbitrary"))(*args)
    return outs


def _ep_plain(accs, xs):
    return accs[0]


def _ep_bias(accs, xs):
    return accs[0] + xs[0]


def _ep_residual(accs, xs):
    return xs[0] + xs[1] * accs[0]


def _make_ep_residual_pair(n_prompt_tiles):
    def ep(accs, xs):
        x = jnp.where(pl.program_id(1) < n_prompt_tiles, xs[0], xs[1])
        return x + xs[2] * accs[0]
    return ep


def _ep_swiglu(accs, xs):
    g = accs[0]
    return (g * jax.nn.sigmoid(g)) * accs[1]


def _ep_rmsnorm(accs, xs):
    acc = accs[0]
    y = acc * lax.rsqrt(jnp.mean(acc * acc, axis=-1, keepdims=True) + EPS) * xs[0]
    return y, y


def _silu(x):
    return x * jax.nn.sigmoid(x)


def _modulate_kernel(*refs, n_prompt_tiles):
    g_ref, sh_ref, sc_ref, o_ref = refs[-4:]
    x = _pick_rows(refs[:-4], n_prompt_tiles, pl.program_id(0))
    y = x * lax.rsqrt(jnp.mean(x * x, axis=-1, keepdims=True) + EPS) * g_ref[...]
    o_ref[...] = (y * (1.0 + sc_ref[...]) + sh_ref[...]).astype(o_ref.dtype)


def _mod_spec(tok, chunk, width, col_of):
    def im(*g):
        i, j = col_of(*g)
        return (tok.mod_row(i) * N_MOD + chunk, 0, j)
    return ((None, 1, width), im)


class _Tokens:
    def __init__(self, n_p, t_p, n_s, t_s):
        self.n_p, self.t_p, self.n_s, self.t_s = n_p, t_p, n_s, t_s
        self.m = n_p + n_s

    def with_tile(self, tm):
        assert self.n_p % tm == 0 and self.t_s % tm == 0
        t = _Tokens(self.n_p, self.t_p, self.n_s, self.t_s)
        t.tm = tm
        return t

    def mod_row(self, i):
        start = i * self.tm
        return jnp.where(start < self.n_p, 0, 1 + (start - self.n_p) // self.t_s)


def modulate(x, g, mod, tok, shift_chunk, scale_chunk, tr=512):
    m, d = tok.m, g.shape[1]
    tk = tok.with_tile(tr)
    npt = tok.n_p // tr
    sh_bs, sh_im = _mod_spec(tk, shift_chunk, d, lambda i: (i, 0))
    sc_bs, sc_im = _mod_spec(tk, scale_chunk, d, lambda i: (i, 0))
    x_specs, x_args = _pair_specs(x, (tr, d), npt, lambda i: (i, 0))
    return pl.pallas_call(
        functools.partial(_modulate_kernel, n_prompt_tiles=npt), grid=(m // tr,),
        in_specs=x_specs + [pl.BlockSpec((1, d), lambda i: (0, 0)),
                            pl.BlockSpec(sh_bs, sh_im), pl.BlockSpec(sc_bs, sc_im)],
        out_specs=pl.BlockSpec((tr, d), lambda i: (i, 0)),
        out_shape=jax.ShapeDtypeStruct((m, d), BF16),
        compiler_params=_params("arbitrary"))(*x_args, g, mod, mod)


MOE_TILE = 512
MOE_ALIGN = 16
CHUNK_BITS = tuple(MOE_TILE >> s for s in range(6))


def _moe_buf_rows(n_experts):
    rows = TOP_K * MOE_TILE + n_experts * (MOE_ALIGN - 1)
    return -(-rows // LANES) * LANES


def _router_modulate_kernel(x_ref, g_ref, sh_ref, sc_ref, wr_ref, o_ref, route_ref, cnt_ref, *, n_experts):
    x = x_ref[...]
    y = x * lax.rsqrt(jnp.mean(x * x, axis=-1, keepdims=True) + EPS) * g_ref[...]
    h = y * (1.0 + sc_ref[...]) + sh_ref[...]
    o_ref[...] = h.astype(o_ref.dtype)
    w = wr_ref[...]
    h1 = h.astype(BF16)
    h2 = (h - h1.astype(F32)).astype(BF16)
    w1 = w.astype(BF16)
    w2 = (w - w1.astype(F32)).astype(BF16)
    logits = (jnp.dot(h1, w1, preferred_element_type=F32) + jnp.dot(h1, w2, preferred_element_type=F32)
              + jnp.dot(h2, w1, preferred_element_type=F32))
    tr = logits.shape[0]
    lane = lax.broadcasted_iota(jnp.int32, logits.shape, 1).astype(F32)
    neg = jnp.float32(-jnp.inf)
    lg = jnp.where(lane < n_experts, logits, neg)
    m1 = jnp.max(lg, axis=-1, keepdims=True)
    i1 = jnp.min(jnp.where(lg == m1, lane, float(LANES)), axis=-1, keepdims=True)
    lg2 = jnp.where(lane == i1, neg, lg)
    m2 = jnp.max(lg2, axis=-1, keepdims=True)
    i2 = jnp.min(jnp.where(lg2 == m2, lane, float(LANES)), axis=-1, keepdims=True)
    e2 = jnp.exp(m2 - m1)
    p1 = 1.0 / (1.0 + e2)
    p2 = e2 / (1.0 + e2)
    ind = jnp.where((lane == i1) | (lane == i2), 1.0, 0.0)
    r = lax.broadcasted_iota(jnp.int32, (tr, tr), 0)
    c = lax.broadcasted_iota(jnp.int32, (tr, tr), 1)
    rank = jnp.dot(jnp.where(c < r, 1.0, 0.0).astype(BF16), ind.astype(BF16), preferred_element_type=F32)
    cnt = jnp.sum(ind, axis=0, keepdims=True)
    cpad = jnp.floor((cnt + (MOE_ALIGN - 1.0)) * (1.0 / MOE_ALIGN)) * MOE_ALIGN
    lane1 = lane[0:1, :]
    off = jnp.zeros_like(cpad)
    for e in range(n_experts - 1):
        off = off + jnp.where(lane1 > e, cpad[:, e:e + 1], 0.0)
    pos = off + rank
    lp1 = jnp.sum(jnp.where(lane == i1, pos, 0.0), axis=-1, keepdims=True)
    lp2 = jnp.sum(jnp.where(lane == i2, pos, 0.0), axis=-1, keepdims=True)
    route_ref[...] = (jnp.where(lane == 0, lp1, 0.0) + jnp.where(lane == 1, lp2, 0.0)
                      + jnp.where(lane == 2, p1, 0.0) + jnp.where(lane == 3, p2, 0.0))
    cnt_ref[...] = jnp.broadcast_to(cpad, cnt_ref.shape)


def router_modulate(x, g, mod, tok, shift_chunk, scale_chunk, w_router_padded, n_experts):
    m, d = x.shape
    tr = MOE_TILE
    tk = tok.with_tile(tr)
    sh_bs, sh_im = _mod_spec(tk, shift_chunk, d, lambda i: (i, 0))
    sc_bs, sc_im = _mod_spec(tk, scale_chunk, d, lambda i: (i, 0))
    h, route, cnt = pl.pallas_call(
        functools.partial(_router_modulate_kernel, n_experts=n_experts), grid=(m // tr,),
        in_specs=[pl.BlockSpec((tr, d), lambda i: (i, 0)), pl.BlockSpec((1, d), lambda i: (0, 0)),
                  pl.BlockSpec(sh_bs, sh_im), pl.BlockSpec(sc_bs, sc_im),
                  pl.BlockSpec((d, LANES), lambda i: (0, 0))],
        out_specs=[pl.BlockSpec((tr, d), lambda i: (i, 0)), pl.BlockSpec((tr, LANES), lambda i: (i, 0)),
                   pl.BlockSpec((8, LANES), lambda i: (i, 0))],
        out_shape=[jax.ShapeDtypeStruct((m, d), BF16), jax.ShapeDtypeStruct((m, LANES), F32),
                   jax.ShapeDtypeStruct((m // tr * 8, LANES), F32)],
        compiler_params=_params("arbitrary"))(x, g, mod, mod, w_router_padded)
    return h, route, cnt[::8, :n_experts].astype(jnp.int32)


def _chunk_copies(t, src_ref, dst_ref, len_ref, n_experts, make_copy, action):
    for e in range(n_experts):
        n = len_ref[t * n_experts + e]
        s = src_ref[t * n_experts + e]
        d = dst_ref[t * n_experts + e]
        off = jnp.int32(0)
        for bit in CHUNK_BITS:
            @pl.when((n & bit) != 0)
            def _(off=off, bit=bit, s=s, d=d):
                cp = make_copy(pl.multiple_of(s + off, MOE_ALIGN), pl.multiple_of(d + off, MOE_ALIGN), bit)
                if action == "start":
                    cp.start()
                else:
                    cp.wait()
            off = off + (n & bit)


def _moe_scatter_kernel(src_ref, dst_ref, len_ref, h_ref, route_ref, init_ref, xs_ref, buf, sem, *, n_experts):
    del init_ref
    t = pl.program_id(0)
    rows = buf.shape[0]
    lane = lax.broadcasted_iota(jnp.int32, (h_ref.shape[0], rows), 1).astype(F32)
    sel = jnp.where((lane == route_ref[:, 0:1]) | (lane == route_ref[:, 1:2]), 1.0, 0.0).astype(BF16)
    buf[...] = lax.dot_general(sel, h_ref[...], (((0,), (0,)), ((), ())),
                               preferred_element_type=F32).astype(buf.dtype)

    def make_copy(s, d, n):
        return pltpu.make_async_copy(buf.at[pl.ds(s, n), :], xs_ref.at[pl.ds(d, n), :], sem)

    _chunk_copies(t, src_ref, dst_ref, len_ref, n_experts, make_copy, "start")
    _chunk_copies(t, src_ref, dst_ref, len_ref, n_experts, make_copy, "wait")


def moe_scatter(h, route, src, dst, ln, n_rows, n_experts):
    m, d = h.shape
    tr = MOE_TILE
    rows = _moe_buf_rows(n_experts)
    init = jnp.zeros((n_rows, d), BF16)
    grid_spec = pltpu.PrefetchScalarGridSpec(
        num_scalar_prefetch=3, grid=(m // tr,),
        in_specs=[pl.BlockSpec((tr, d), lambda i, *_: (i, 0)), pl.BlockSpec((tr, LANES), lambda i, *_: (i, 0)),
                  pl.BlockSpec(memory_space=pl.ANY)],
        out_specs=pl.BlockSpec(memory_space=pl.ANY),
        scratch_shapes=[pltpu.VMEM((rows, d), BF16), pltpu.SemaphoreType.DMA(())])
    return pl.pallas_call(
        functools.partial(_moe_scatter_kernel, n_experts=n_experts), grid_spec=grid_spec,
        out_shape=jax.ShapeDtypeStruct((n_rows, d), BF16), input_output_aliases={5: 0},
        compiler_params=_params("arbitrary"))(src, dst, ln, h, route, init)


def _gmm_kernel(te_ref, first_ref, na_ref, a_ref, *refs, n_g, epilogue):
    w_refs = refs[:n_g]
    o_ref = refs[n_g]
    wb_refs = refs[n_g + 1:]
    j = pl.program_id(1)

    @pl.when(first_ref[j] == 1)
    def _():
        for w, wb in zip(w_refs, wb_refs):
            wb[...] = w[...].astype(BF16)

    @pl.when(j < na_ref[0])
    def _():
        a = a_ref[...]
        accs = [jnp.dot(a, wb[...], preferred_element_type=F32) for wb in wb_refs]
        o_ref[...] = epilogue(accs, []).astype(o_ref.dtype)

    @pl.when(j >= na_ref[0])
    def _():
        o_ref[...] = jnp.zeros_like(o_ref)


def grouped_matmul(a, ws, te, first, na, *, tn, epilogue, w_buffers=2):
    rows, k = a.shape
    n = ws[0].shape[2]
    tm = MOE_TILE

    def row_map(jn, j, te, first, na):
        return (jnp.minimum(j, na[0] - 1), 0)

    def w_map(jn, j, te, first, na):
        return (te[j], 0, jn)

    def out_map(jn, j, te, first, na):
        return (j, jn)

    w_kwargs = {} if w_buffers == 2 else {"pipeline_mode": pl.Buffered(w_buffers)}
    grid_spec = pltpu.PrefetchScalarGridSpec(
        num_scalar_prefetch=3, grid=(n // tn, rows // tm),
        in_specs=[pl.BlockSpec((tm, k), row_map)] + [pl.BlockSpec((None, k, tn), w_map, **w_kwargs) for _ in ws],
        out_specs=pl.BlockSpec((tm, tn), out_map),
        scratch_shapes=[pltpu.VMEM((k, tn), BF16) for _ in ws])
    return pl.pallas_call(
        functools.partial(_gmm_kernel, n_g=len(ws), epilogue=epilogue), grid_spec=grid_spec,
        out_shape=jax.ShapeDtypeStruct((rows, n), BF16),
        compiler_params=_params("arbitrary", "arbitrary"))(te, first, na, a, *ws)


def _moe_combine_kernel(src_ref, dst_ref, len_ref, x_ref, route_ref, gate_ref, gf_ref, ys_ref, op_ref, os_ref,
                        buf, sem, *, n_experts, n_prompt_tiles):
    t = pl.program_id(0)

    @pl.when(t == 0)
    def _():
        buf[...] = jnp.zeros_like(buf)

    def make_copy(s, d, n):
        return pltpu.make_async_copy(ys_ref.at[pl.ds(d, n), :], buf.at[pl.ds(s, n), :], sem)

    _chunk_copies(t, src_ref, dst_ref, len_ref, n_experts, make_copy, "start")
    _chunk_copies(t, src_ref, dst_ref, len_ref, n_experts, make_copy, "wait")
    rows = buf.shape[0]
    lane = lax.broadcasted_iota(jnp.int32, (x_ref.shape[0], rows), 1).astype(F32)
    y = buf[...]
    s1 = jnp.where(lane == route_ref[:, 0:1], 1.0, 0.0).astype(BF16)
    s2 = jnp.where(lane == route_ref[:, 1:2], 1.0, 0.0).astype(BF16)
    f = (route_ref[:, 2:3] * jnp.dot(s1, y, preferred_element_type=F32)
         + route_ref[:, 3:4] * jnp.dot(s2, y, preferred_element_type=F32))
    x = x_ref[...] + gate_ref[...] * f
    y = x * lax.rsqrt(jnp.mean(x * x, axis=-1, keepdims=True) + EPS) * gf_ref[...]

    @pl.when(t < n_prompt_tiles)
    def _():
        op_ref[...] = y

    @pl.when(t >= n_prompt_tiles)
    def _():
        os_ref[...] = y


def moe_combine(x, route, mod, tok, gate_chunk, g_final, ys, src, dst, ln, n_experts):
    m, d = x.shape
    tr = MOE_TILE
    rows = _moe_buf_rows(n_experts)
    tk = tok.with_tile(tr)
    npt = tok.n_p // tr
    g_bs, g_im = _mod_spec(tk, gate_chunk, d, lambda i, *_: (i, 0))
    grid_spec = pltpu.PrefetchScalarGridSpec(
        num_scalar_prefetch=3, grid=(m // tr,),
        in_specs=[pl.BlockSpec((tr, d), lambda i, *_: (i, 0)), pl.BlockSpec((tr, LANES), lambda i, *_: (i, 0)),
                  pl.BlockSpec(g_bs, g_im), pl.BlockSpec((1, d), lambda i, *_: (0, 0)),
                  pl.BlockSpec(memory_space=pl.ANY)],
        out_specs=[pl.BlockSpec((tr, d), lambda i, *_: (jnp.minimum(i, npt - 1), 0)),
                   pl.BlockSpec((tr, d), lambda i, *_: (jnp.maximum(i - npt, 0), 0))],
        scratch_shapes=[pltpu.VMEM((rows, d), BF16), pltpu.SemaphoreType.DMA(())])
    return pl.pallas_call(
        functools.partial(_moe_combine_kernel, n_experts=n_experts, n_prompt_tiles=npt), grid_spec=grid_spec,
        out_shape=[jax.ShapeDtypeStruct((tok.n_p, d), F32), jax.ShapeDtypeStruct((tok.n_s, d), F32)],
        compiler_params=_params("arbitrary"))(src, dst, ln, x, route, mod, g_final, ys)


def _moe_plan(cpad, n_tiles_sorted):
    n_t, n_e = cpad.shape
    tot = jnp.sum(cpad, axis=0)
    region = (tot + MOE_TILE - 1) // MOE_TILE * MOE_TILE
    g_end = jnp.cumsum(region)
    dst = (g_end - region)[None, :] + jnp.cumsum(cpad, axis=0) - cpad
    src = jnp.cumsum(cpad, axis=1) - cpad
    start = jnp.arange(n_tiles_sorted, dtype=jnp.int32) * MOE_TILE
    active = start < g_end[-1]
    te = jnp.sum((start[:, None] >= g_end[None, :]).astype(jnp.int32), axis=1)
    te = jnp.where(active, te, jnp.max(jnp.where(active, te, 0)))
    first = active & jnp.concatenate([jnp.ones((1,), bool), te[1:] != te[:-1]])
    na = (g_end[-1:] // MOE_TILE).astype(jnp.int32)
    i32 = lambda v: v.reshape(-1).astype(jnp.int32)
    return i32(src), i32(dst), i32(cpad), i32(te), i32(first), na


def _rmsnorm_kernel(x_ref, g_ref, o_ref):
    x = x_ref[...]
    o_ref[...] = x * lax.rsqrt(jnp.mean(x * x, axis=-1, keepdims=True) + EPS) * g_ref[...]


def rmsnorm_rows(x, g, tr=512):
    m, d = x.shape
    return pl.pallas_call(
        _rmsnorm_kernel, grid=(m // tr,),
        in_specs=[pl.BlockSpec((tr, d), lambda i: (i, 0)), pl.BlockSpec((1, d), lambda i: (0, 0))],
        out_specs=pl.BlockSpec((tr, d), lambda i: (i, 0)),
        out_shape=jax.ShapeDtypeStruct((m, d), F32),
        compiler_params=_params("arbitrary"))(x, g)


def _make_conv_epilogue(tok, tm):
    def ep(accs, xs):
        ux, ub, uc = accs
        cw = xs[0]
        p = uc * ux
        i = pl.program_id(1)
        r = lax.broadcasted_iota(jnp.int32, p.shape, 0)
        is_p = i * tm < tok.n_p
        t = jnp.where(is_p, r % tok.t_p, r % tok.t_s)
        t_len = jnp.where(is_p, tok.t_p, tok.t_s)
        prev = jnp.where(t == 0, 0.0, pltpu.roll(p, 1, axis=0))
        nxt = jnp.where(t == t_len - 1, 0.0, pltpu.roll(p, tm - 1, axis=0))
        return ub * (prev * cw[0:1, :] + p * cw[1:2, :] + nxt * cw[2:3, :])
    return ep


def _swap16(x):
    n = x.shape[-1]
    lane = lax.broadcasted_iota(jnp.int32, x.shape, x.ndim - 1)
    fwd = pltpu.roll(x, n - 16, axis=x.ndim - 1)
    bwd = pltpu.roll(x, 16, axis=x.ndim - 1)
    return jnp.where((lane // 16) % 2 == 0, fwd, bwd)


def _attn_kernel(*refs, n_heads, with_ctx, scale):
    if with_ctx:
        (qn_ref, qr_ref, kv_ref, kr_ref, ckv_ref, ckr_ref, cq_ref, sq_ref, ck_ref, sk_ref, o_ref) = refs
    else:
        qn_ref, qr_ref, kv_ref, kr_ref, o_ref = refs
    qr = qr_ref[...]
    kr = kr_ref[...][:, :QK_ROPE]
    if with_ctx:
        qr = qr * cq_ref[...] + _swap16(qr) * sq_ref[...]
        kr128 = kr_ref[...]
        kr = (kr128 * ck_ref[...] + _swap16(kr128) * sk_ref[...])[:, :QK_ROPE]
    qr = qr.astype(BF16)
    kr = kr.astype(BF16)
    if with_ctx:
        ckr = ckr_ref[...].astype(BF16)
    dn = (((1,), (1,)), ((), ()))
    for h in range(n_heads):
        q = jnp.concatenate([qn_ref[:, h * QK_NOPE:(h + 1) * QK_NOPE],
                             qr[:, h * QK_ROPE:(h + 1) * QK_ROPE]], axis=1)
        c0 = h * (QK_NOPE + V_HEAD)
        k = jnp.concatenate([kv_ref[:, c0:c0 + QK_NOPE], kr], axis=1)
        s = lax.dot_general(q, k, dn, preferred_element_type=F32) * scale
        mx = jnp.max(s, axis=-1, keepdims=True)
        if with_ctx:
            kc = jnp.concatenate([ckv_ref[:, c0:c0 + QK_NOPE], ckr], axis=1)
            s2 = lax.dot_general(q, kc, dn, preferred_element_type=F32) * scale
            mx = jnp.maximum(mx, jnp.max(s2, axis=-1, keepdims=True))
        p = jnp.exp(s - mx)
        den = jnp.sum(p, axis=-1, keepdims=True)
        o = jnp.dot(p.astype(BF16), kv_ref[:, c0 + QK_NOPE:c0 + QK_NOPE + V_HEAD],
                    preferred_element_type=F32)
        if with_ctx:
            p2 = jnp.exp(s2 - mx)
            den = den + jnp.sum(p2, axis=-1, keepdims=True)
            o = o + jnp.dot(p2.astype(BF16), ckv_ref[:, c0 + QK_NOPE:c0 + QK_NOPE + V_HEAD],
                            preferred_element_type=F32)
        o_ref[:, h * V_HEAD:(h + 1) * V_HEAD] = (o / den).astype(o_ref.dtype)


def attention(qn, qr, kv, kr, *, row0, n_batch, t, tq, ctx=None):
    h = MLA_HEADS
    nq = t // tq
    qb0 = row0 // tq
    kb0 = row0 // t
    scale = float(QK_NOPE + QK_ROPE) ** -0.5
    in_specs = [pl.BlockSpec((tq, h * QK_NOPE), lambda b, i: (qb0 + b * nq + i, 0)),
                pl.BlockSpec((tq, h * QK_ROPE), lambda b, i: (qb0 + b * nq + i, 0)),
                pl.BlockSpec((t, h * (QK_NOPE + V_HEAD)), lambda b, i: (kb0 + b, 0)),
                pl.BlockSpec((t, LANES), lambda b, i: (kb0 + b, 0))]
    args = [qn, qr, kv, kr]
    if ctx is not None:
        ckv, ckr, past, (cq, sq, ck, sk) = ctx
        in_specs += [pl.BlockSpec((past, h * (QK_NOPE + V_HEAD)), lambda b, i: (b, 0)),
                     pl.BlockSpec((past, QK_ROPE), lambda b, i: (b, 0)),
                     pl.BlockSpec((tq, h * QK_ROPE), lambda b, i: (i, 0)),
                     pl.BlockSpec((tq, h * QK_ROPE), lambda b, i: (i, 0)),
                     pl.BlockSpec((t, LANES), lambda b, i: (0, 0)),
                     pl.BlockSpec((t, LANES), lambda b, i: (0, 0))]
        args += [ckv, ckr, cq, sq, ck, sk]
    return pl.pallas_call(
        functools.partial(_attn_kernel, n_heads=h, with_ctx=ctx is not None, scale=scale),
        grid=(n_batch, nq), in_specs=in_specs,
        out_specs=pl.BlockSpec((tq, h * V_HEAD), lambda b, i: (b * nq + i, 0)),
        out_shape=jax.ShapeDtypeStruct((n_batch * t, h * V_HEAD), BF16),
        compiler_params=_params("arbitrary", "arbitrary"))(*args)


def _rope_tables(t):
    rows = t // GRID_W
    row = jnp.repeat(jnp.arange(rows, dtype=F32), GRID_W)
    col = jnp.tile(jnp.arange(GRID_W, dtype=F32), rows)
    half = QK_ROPE // 2
    inv_freq = ROPE_BASE ** (-jnp.arange(0, half, 2, dtype=F32) / half)
    ang_r = row[:, None] * inv_freq
    ang_c = col[:, None] * inv_freq
    cr, sr, cc, sc = jnp.cos(ang_r), jnp.sin(ang_r), jnp.cos(ang_c), jnp.sin(ang_c)
    cos = jnp.concatenate([cr, cr, cc, cc], axis=-1)
    sin = jnp.concatenate([-sr, sr, -sc, sc], axis=-1)
    return cos, sin


def _log_sigmoid(x):
    return jnp.minimum(x, 0.0) - jnp.log(1.0 + jnp.exp(-jnp.abs(x)))


def _mlstm_kernel(*refs, chunk, n_heads, dk, dv, has_init):
    if has_init:
        (q_ref, k_ref, v_ref, g_ref, gt_ref, bg_ref, bgt_ref, c0_ref, n0_ref, m0_ref,
         h_ref, c_ref, n_ref, m_ref) = refs
    else:
        q_ref, k_ref, v_ref, g_ref, gt_ref, bg_ref, bgt_ref, h_ref, c_ref, n_ref, m_ref = refs
    d = pl.program_id(1)

    @pl.when(pl.program_id(2) == 0)
    def _():
        if has_init:
            c_ref[...] = c0_ref[...]
            n_ref[...] = n0_ref[...]
            m_ref[...] = m0_ref[...]
        else:
            c_ref[...] = jnp.zeros_like(c_ref)
            n_ref[...] = jnp.zeros_like(n_ref)
            m_ref[...] = jnp.zeros_like(m_ref)

    L = chunk
    row = lax.broadcasted_iota(jnp.int32, (L, L), 0)
    col = lax.broadcasted_iota(jnp.int32, (L, L), 1)
    sgn = jnp.where(d == 0, 1, -1)
    mask = (col - row) * sgn <= 0
    mask_t = (row - col) * sgn <= 0
    neg = jnp.float32(-jnp.inf)
    g = g_ref[...] + bg_ref[...]
    gt = gt_ref[...] + bgt_ref[...]
    dn_t = (((0,), (0,)), ((), ()))
    for h in range(n_heads):
        i_col = g[:, h:h + 1]
        f_col = _log_sigmoid(g[:, n_heads + h:n_heads + h + 1])
        i_row = gt[h:h + 1, :]
        f_row = _log_sigmoid(gt[n_heads + h:n_heads + h + 1, :])
        b_col = jnp.sum(jnp.where(mask, f_row, 0.0), axis=1, keepdims=True)
        b_row = jnp.sum(jnp.where(mask_t, f_col, 0.0), axis=0, keepdims=True)
        m_old = m_ref[:, h:h + 1]
        dmat = jnp.where(mask, b_col - b_row + i_row, neg)
        a_col = b_col + m_old
        mrow = jnp.maximum(a_col, jnp.max(dmat, axis=1, keepdims=True))
        w_intra = jnp.exp(dmat - mrow)
        w_inter = jnp.exp(a_col - mrow)
        qh = q_ref[:, h * dk:(h + 1) * dk]
        kh = k_ref[:, h * dk:(h + 1) * dk] * (dk ** -0.5)
        vh = v_ref[:, h * dv:(h + 1) * dv]
        c_old = c_ref[h]
        n_old = n_ref[h:h + 1, :]
        s = lax.dot_general(qh, kh, (((1,), (1,)), ((), ())), preferred_element_type=F32) * w_intra
        num = (w_inter * jnp.dot(qh, c_old.astype(BF16), preferred_element_type=F32)
               + jnp.dot(s.astype(BF16), vh, preferred_element_type=F32))
        qn = jnp.sum(qh.astype(F32) * n_old, axis=1, keepdims=True)
        den = w_inter * qn + jnp.sum(s, axis=1, keepdims=True)
        h_ref[:, h * dv:(h + 1) * dv] = num / jnp.maximum(jnp.abs(den), jnp.exp(-mrow))
        bl = jnp.sum(f_row, axis=1, keepdims=True)
        g_row = bl - b_row + i_row
        g_col = bl - b_col + i_col
        m_new = jnp.maximum(bl + m_old, jnp.max(g_row, axis=1, keepdims=True))
        wg_col = jnp.exp(g_col - m_new)
        decay = jnp.exp(bl + m_old - m_new)
        kw = kh.astype(F32) * wg_col
        c_ref[h] = decay * c_old + lax.dot_general(kw.astype(BF16), vh, dn_t, preferred_element_type=F32)
        n_ref[h:h + 1, :] = decay * n_old + jnp.sum(kw, axis=0, keepdims=True)
        m_ref[:, h:h + 1] = m_new


def mlstm(qkv, gates, gates_t, bg, bg_t, *, row0, n_batch, t, chunk, init=None):
    nh = ML_HEADS
    hv = qkv.shape[1] // 2
    dv = hv // nh
    dk = dv // 2
    hk = nh * dk
    nc = t // chunk
    rb0 = row0 // chunk

    def blk(b, d, c):
        return rb0 + b * nc + jnp.where(d == 0, c, nc - 1 - c)

    in_specs = [pl.BlockSpec((chunk, hk), lambda b, d, c: (blk(b, d, c), 0)),
                pl.BlockSpec((chunk, hk), lambda b, d, c: (blk(b, d, c), 1)),
                pl.BlockSpec((chunk, hv), lambda b, d, c: (blk(b, d, c), 1)),
                pl.BlockSpec((None, chunk, 2 * nh), lambda b, d, c: (d, blk(b, d, c), 0)),
                pl.BlockSpec((None, 2 * nh, chunk), lambda b, d, c: (d, 0, blk(b, d, c))),
                pl.BlockSpec((None, 1, 2 * nh), lambda b, d, c: (d, 0, 0)),
                pl.BlockSpec((None, 2 * nh, 1), lambda b, d, c: (d, 0, 0))]
    args = [qkv, qkv, qkv, gates, gates_t, bg, bg_t]
    if init is not None:
        in_specs += [pl.BlockSpec((None, None, nh, dk, dv), lambda b, d, c: (b, d, 0, 0, 0)),
                     pl.BlockSpec((None, None, nh, dk), lambda b, d, c: (b, d, 0, 0)),
                     pl.BlockSpec((None, None, 1, nh), lambda b, d, c: (b, d, 0, 0))]
        args += list(init)
    out_specs = [pl.BlockSpec((None, chunk, hv), lambda b, d, c: (d, blk(b, d, c) - rb0, 0)),
                 pl.BlockSpec((None, None, nh, dk, dv), lambda b, d, c: (b, d, 0, 0, 0)),
                 pl.BlockSpec((None, None, nh, dk), lambda b, d, c: (b, d, 0, 0)),
                 pl.BlockSpec((None, None, 1, nh), lambda b, d, c: (b, d, 0, 0))]
    out_shape = [jax.ShapeDtypeStruct((2, n_batch * t, hv), F32),
                 jax.ShapeDtypeStruct((n_batch, 2, nh, dk, dv), F32),
                 jax.ShapeDtypeStruct((n_batch, 2, nh, dk), F32),
                 jax.ShapeDtypeStruct((n_batch, 2, 1, nh), F32)]
    return pl.pallas_call(
        functools.partial(_mlstm_kernel, chunk=chunk, n_heads=nh, dk=dk, dv=dv, has_init=init is not None),
        grid=(n_batch, 2, nc), in_specs=in_specs, out_specs=out_specs, out_shape=out_shape,
        compiler_params=_params("arbitrary", "arbitrary", "arbitrary"))(*args)


def _mlstm_out_kernel(hfp_ref, hbp_ref, hfs_ref, hbs_ref, o_ref, g_ref, y_ref, *, n_heads, dv, n_prompt_tiles):
    is_p = pl.program_id(0) < n_prompt_tiles
    for h in range(n_heads):
        sl = slice(h * dv, (h + 1) * dv)
        x = jnp.where(is_p, hfp_ref[:, sl] + hbp_ref[:, sl], hfs_ref[:, sl] + hbs_ref[:, sl])
        ht = x * lax.rsqrt(jnp.mean(x * x, axis=-1, keepdims=True) + EPS) * g_ref[:, sl]
        y_ref[:, sl] = (jax.nn.sigmoid(o_ref[:, sl]) * ht).astype(y_ref.dtype)


def mlstm_out(h_p, h_s, o, g_h, tr=256):
    hv = h_p.shape[2]
    npt = h_p.shape[1] // tr
    m = h_p.shape[1] + h_s.shape[1]

    def p_spec(d):
        return pl.BlockSpec((None, tr, hv), lambda i: (d, jnp.minimum(i, npt - 1), 0))

    def s_spec(d):
        return pl.BlockSpec((None, tr, hv), lambda i: (d, jnp.maximum(i - npt, 0), 0))

    return pl.pallas_call(
        functools.partial(_mlstm_out_kernel, n_heads=ML_HEADS, dv=hv // ML_HEADS, n_prompt_tiles=npt),
        grid=(m // tr,),
        in_specs=[p_spec(0), p_spec(1), s_spec(0), s_spec(1),
                  pl.BlockSpec((tr, hv), lambda i: (i, 0)),
                  pl.BlockSpec((1, hv), lambda i: (0, 0))],
        out_specs=pl.BlockSpec((tr, hv), lambda i: (i, 0)),
        out_shape=jax.ShapeDtypeStruct((m, hv), BF16),
        compiler_params=_params("arbitrary"))(h_p, h_p, h_s, h_s, o, g_h)


def _adaln(cond, w, b):
    rows, d = cond.shape
    n = w.shape[1]
    tn = n // 8
    out, = matmul([(cond, None)], [[(w, 0, 0)]], [F32], n_cols=n, tm=rows, tn=tn, epilogue=_ep_bias,
                  extras=[(b.reshape(1, n), (1, tn), lambda j, i: (0, j))], prologue=_silu)
    return out.reshape(rows * N_MOD, 1, d)


def _residual_extras(x, mod, tok, tm, tn, chunk):
    tk = tok.with_tile(tm)
    bs, im = _mod_spec(tk, chunk, tn, lambda j, i: (i, j))
    specs, arrs = _pair_specs(x, (tm, tn), tok.n_p // tm, lambda j, i: (i, j))
    return [(a, s.block_shape, s.index_map) for a, s in zip(arrs, specs)] + [(mod, bs, im)]


def kernel(x_prompt, x_sample, c, cache_ckv, cache_krope, state_C, state_n, state_m, c_ctx, w_mod, b_mod, g_mix, g_ffn, w_in_a, g_q, g_kv, w_uq, w_ukv, conv_w, w_o_a, w_in_c, b_gates, g_h, w_o_c, w_ffn_gate, w_ffn_up, w_ffn_down, w_router, w_exp_gate, w_exp_up, w_exp_down, g_final):
    bp, tp, d = x_prompt.shape
    bs, ts, _ = x_sample.shape
    past = cache_ckv.shape[2]
    n_p, n_s = bp * tp, bs * ts
    tok = _Tokens(n_p, tp, n_s, ts)
    m = tok.m
    assert n_p % ts == 0 and ts % 512 == 0 and tp % 256 == 0
    x = (x_prompt.reshape(n_p, d), x_sample.reshape(n_s, d))

    n_cond = -(-(1 + bs) // 8) * 8
    cond = jnp.concatenate([c_ctx[None], c, jnp.zeros((n_cond - 1 - bs, d), F32)], axis=0)

    TM, TN = 512, 512

    mod = _adaln(cond, w_mod[0], b_mod[0])
    h = modulate(x, g_mix[0].reshape(1, d), mod, tok, 0, 1)
    w_in = w_in_a[0]
    conv_dim = conv_w.shape[2]
    c_kr = Q_LORA + KV_LORA
    c_ux = c_kr + QK_ROPE
    gq_kv = jnp.concatenate([g_q[0], g_kv[0]])[None]
    lat_f32, lat_bf = matmul([(h, None)], [[(w_in, 0, 0)]], [F32, BF16], n_cols=c_kr, tm=TM, tn=Q_LORA,
                             epilogue=_ep_rmsnorm, extras=[(gq_kv, (1, Q_LORA), lambda j, i: (0, j))])
    w_kr = jnp.pad(w_in[:, c_kr:c_ux], ((0, 0), (0, LANES - QK_ROPE)))
    kr, = matmul([(h, None)], [[(w_kr, 0, 0)]], [F32], n_cols=LANES, tm=TM, tn=LANES, epilogue=_ep_plain)
    w_conv = w_in[:, c_ux:]
    cb = 256
    ncb = conv_dim // cb
    conv, = matmul([(h, None)], [[(w_conv, 0, g * ncb)] for g in range(3)], [BF16], n_cols=conv_dim,
                   tm=ts, tn=cb, epilogue=_make_conv_epilogue(tok, ts),
                   extras=[(conv_w[0], (CONV_W, cb), lambda j, i: (0, j))])
    wq = w_uq[0].reshape(Q_LORA, MLA_HEADS, QK_NOPE + QK_ROPE)
    wq_n = wq[:, :, :QK_NOPE].reshape(Q_LORA, MLA_HEADS * QK_NOPE)
    wq_r = wq[:, :, QK_NOPE:].reshape(Q_LORA, MLA_HEADS * QK_ROPE)
    qn, = matmul([(lat_bf, (0, Q_LORA))], [[(wq_n, 0, 0)]], [BF16], n_cols=MLA_HEADS * QK_NOPE, tm=TM, tn=TN,
                 epilogue=_ep_plain)
    qr, = matmul([(lat_bf, (0, Q_LORA))], [[(wq_r, 0, 0)]], [F32], n_cols=MLA_HEADS * QK_ROPE, tm=TM, tn=TN,
                 epilogue=_ep_plain)
    n_kv = MLA_HEADS * (QK_NOPE + V_HEAD)
    kv, = matmul([(lat_bf, (1, KV_LORA))], [[(w_ukv[0], 0, 0)]], [BF16], n_cols=n_kv, tm=TM, tn=TN,
                 epilogue=_ep_plain)
    ctx_kv, = matmul([(cache_ckv[:, 0].reshape(bs * past, KV_LORA), None)], [[(w_ukv[0], 0, 0)]], [BF16],
                     n_cols=n_kv, tm=TM, tn=TN, epilogue=_ep_plain)
    ctx_kr = cache_krope[:, 0].reshape(bs * past, QK_ROPE)
    cos, sin = _rope_tables(ts)
    pad = ((0, 0), (0, LANES - QK_ROPE))
    tables = (jnp.tile(cos, (1, MLA_HEADS)), jnp.tile(sin, (1, MLA_HEADS)), jnp.pad(cos, pad), jnp.pad(sin, pad))
    att_p = attention(qn, qr, kv, kr, row0=0, n_batch=bp, t=tp, tq=tp)
    att_s = attention(qn, qr, kv, kr, row0=n_p, n_batch=bs, t=ts, tq=256, ctx=(ctx_kv, ctx_kr, past, tables))
    assert MLA_HEADS * V_HEAD == conv_dim
    x, = matmul([((att_p, att_s), None), (conv, None)], [[(w_o_a[0], 0, 0), (w_o_a[0], 1, 0)]], [F32], n_cols=d,
                tm=TM, tn=TN, epilogue=_make_ep_residual_pair(n_p // TM), n_prompt_rows=n_p,
                extras=_residual_extras(x, mod, tok, TM, TN, 2))
    h = modulate(x, g_ffn[0].reshape(1, d), mod, tok, 3, 4)
    d_ff = w_ffn_gate.shape[2]
    hid, = matmul([(h, None)], [[(w_ffn_gate[0], 0, 0)], [(w_ffn_up[0], 0, 0)]], [BF16], n_cols=d_ff, tm=TM,
                  tn=TN, epilogue=_ep_swiglu)
    x, = matmul([(hid, None)], [[(w_ffn_down[0], 0, 0)]], [F32], n_cols=d, tm=TM, tn=TN, w_buffers=1,
                epilogue=_ep_residual, extras=_residual_extras(x, mod, tok, TM, TN, 5))
    ckv_p = lat_f32[:n_p, Q_LORA:].reshape(bp, 1, tp, KV_LORA)
    kr_p = kr[:n_p, :QK_ROPE].reshape(bp, 1, tp, QK_ROPE)

    mod = _adaln(cond, w_mod[1], b_mod[1])
    h = modulate(x, g_mix[1].reshape(1, d), mod, tok, 0, 1)
    w_c = w_in_c[0]
    hv = d
    hk = hv // 2
    n_qkv = 2 * hk + hv
    qkv, = matmul([(h, None)], [[(w_c, 0, 0)]], [BF16], n_cols=n_qkv, tm=TM, tn=TN, epilogue=_ep_plain)
    o_gate, = matmul([(h, None)], [[(w_c, 0, n_qkv // TN)]], [F32], n_cols=hv, tm=TM, tn=TN, epilogue=_ep_plain)
    n_g = 4 * ML_HEADS
    w_g = jnp.pad(w_c[:, n_qkv + hv:], ((0, 0), (0, LANES - n_g)))
    graw, = matmul([(h, None)], [[(w_g, 0, 0)]], [F32], n_cols=LANES, tm=TM, tn=LANES, epilogue=_ep_plain)
    gates = graw[:, :n_g].reshape(m, 2, 2 * ML_HEADS).transpose(1, 0, 2)
    gates_t = gates.transpose(0, 2, 1)
    bg = b_gates[0].reshape(2, 1, 2 * ML_HEADS)
    bg_t = bg.transpose(0, 2, 1)
    chunk = 128
    h_p, c_p, n_p_state, m_p = mlstm(qkv, gates, gates_t, bg, bg_t, row0=0, n_batch=bp, t=tp, chunk=chunk)
    init = (state_C[:, 0], state_n[:, 0], state_m[:, 0].reshape(bs, 2, 1, ML_HEADS))
    h_s, _, _, _ = mlstm(qkv, gates, gates_t, bg, bg_t, row0=n_p, n_batch=bs, t=ts, chunk=chunk, init=init)
    y = mlstm_out(h_p, h_s, o_gate, g_h[0].reshape(1, hv))
    x, = matmul([(y, None)], [[(w_o_c[0], 0, 0)]], [F32], n_cols=d, tm=TM, tn=TN, epilogue=_ep_residual,
                extras=_residual_extras(x, mod, tok, TM, TN, 2))
    n_exp = w_router.shape[2]
    w_r = jnp.pad(w_router[0], ((0, 0), (0, LANES - n_exp)))
    h, route, cpad = router_modulate(x, g_ffn[1].reshape(1, d), mod, tok, 3, 4, w_r, n_exp)
    n_tt = m // MOE_TILE
    n_sorted = -(-(TOP_K * m + n_tt * n_exp * (MOE_ALIGN - 1) + n_exp * (MOE_TILE - 1)) // MOE_TILE)
    src, dst, ln, te, first, na = _moe_plan(cpad, n_sorted)
    xs = moe_scatter(h, route, src, dst, ln, n_sorted * MOE_TILE, n_exp)
    hid = grouped_matmul(xs, [w_exp_gate[0], w_exp_up[0]], te, first, na, tn=TN, epilogue=_ep_swiglu)
    ys = grouped_matmul(hid, [w_exp_down[0]], te, first, na, tn=TN, epilogue=_ep_plain)
    y_p, y_s = moe_combine(x, route, mod, tok, 5, g_final.reshape(1, d), ys, src, dst, ln, n_exp)
    y_prompt = y_p.reshape(bp, tp, d)
    y_sample = y_s.reshape(bs, ts, d)
    new_c = c_p[:, None]
    new_n = n_p_state[:, None]
    new_m = m_p.reshape(bp, 1, 2, ML_HEADS)
    return (y_prompt, y_sample, ckv_p, kr_p, new_c, new_n, new_m)
```

```python
import functools

import jax
import jax.numpy as jnp
from jax import lax
from jax.experimental import pallas as pl
from jax.experimental.pallas import tpu as pltpu

BF16 = jnp.bfloat16
F32 = jnp.float32

EPS = 1e-6
N_MOD = 6
GRID_W = 64
ROPE_BASE = 10000.0
MLA_HEADS = 8
Q_LORA = 512
KV_LORA = 512
QK_NOPE = 128
QK_ROPE = 64
V_HEAD = 128
CONV_W = 3
ML_HEADS = 4
TOP_K = 2

LANES = 128
VMEM_LIMIT = 56 * 1024 * 1024


def _params(*sem):
    return pltpu.CompilerParams(dimension_semantics=sem, vmem_limit_bytes=VMEM_LIMIT)


def _pick_rows(refs, n_prompt_tiles, i):
    if len(refs) == 1:
        return refs[0][...]
    return jnp.where(i < n_prompt_tiles, refs[0][...], refs[1][...])


def _pair_specs(arr, block, n_prompt_tiles, col_of):
    if not isinstance(arr, tuple):
        return [pl.BlockSpec(block, lambda *g: col_of(*g))], [arr]
    npt = n_prompt_tiles

    def im_p(*g):
        i, c = col_of(*g)
        return (jnp.minimum(i, npt - 1), c)

    def im_s(*g):
        i, c = col_of(*g)
        return (jnp.maximum(i - npt, 0), c)

    return [pl.BlockSpec(block, im_p), pl.BlockSpec(block, im_s)], list(arr)


def _mm_kernel(*refs, a_counts, n_g, n_x, n_o, prologue, epilogue, n_prompt_tiles):
    n_a = len(a_counts)
    n_ar = sum(a_counts)
    a_refs = refs[:n_ar]
    w_refs = refs[n_ar:n_ar + n_a * n_g]
    x_refs = refs[n_ar + n_a * n_g:n_ar + n_a * n_g + n_x]
    o_refs = refs[n_ar + n_a * n_g + n_x:n_ar + n_a * n_g + n_x + n_o]
    wb_refs = refs[n_ar + n_a * n_g + n_x + n_o:]

    @pl.when(pl.program_id(1) == 0)
    def _():
        for w, wb in zip(w_refs, wb_refs):
            wb[...] = w[...].astype(BF16)

    a_vals = []
    pos = 0
    for cnt in a_counts:
        v = _pick_rows(a_refs[pos:pos + cnt], n_prompt_tiles, pl.program_id(1))
        pos += cnt
        if prologue is not None:
            v = prologue(v)
        a_vals.append(v.astype(BF16))
    accs = []
    for g in range(n_g):
        acc = None
        for i in range(n_a):
            d = jnp.dot(a_vals[i], wb_refs[g * n_a + i][...], preferred_element_type=F32)
            acc = d if acc is None else acc + d
        accs.append(acc)
    outs = epilogue(accs, [x[...] for x in x_refs])
    if not isinstance(outs, (tuple, list)):
        outs = (outs,)
    for o, v in zip(o_refs, outs):
        o[...] = v.astype(o.dtype)


def matmul(a_list, w_groups, out_dtypes, *, n_cols, tm, tn, epilogue, extras=(), prologue=None,
           out_tn=None, n_prompt_rows=0, w_buffers=2):
    n_g = len(w_groups)
    npt = n_prompt_rows // tm
    out_tn = tn if out_tn is None else out_tn
    ks, a_counts = [], []
    in_specs, args = [], []
    for arr, kb in a_list:
        first = arr[0] if isinstance(arr, tuple) else arr
        k = first.shape[1] if kb is None else kb[1]
        kidx = 0 if kb is None else kb[0]
        ks.append(k)
        specs, arrs = _pair_specs(arr, (tm, k), npt, lambda j, i, kidx=kidx: (i, kidx))
        in_specs += specs
        args += arrs
        a_counts.append(len(arrs))
    arr0 = a_list[0][0]
    m = sum(a.shape[0] for a in arr0) if isinstance(arr0, tuple) else arr0.shape[0]
    w_kwargs = {} if w_buffers == 2 else {"pipeline_mode": pl.Buffered(w_buffers)}
    scratch = []
    for grp in w_groups:
        for (arr, rb, cb, *lead), k in zip(grp, ks):
            if lead:
                spec = pl.BlockSpec((None, k, tn), lambda j, i, rb=rb, cb=cb, l=lead[0]: (l, rb, j + cb), **w_kwargs)
            else:
                spec = pl.BlockSpec((k, tn), lambda j, i, rb=rb, cb=cb: (rb, j + cb), **w_kwargs)
            in_specs.append(spec)
            args.append(arr)
            scratch.append(pltpu.VMEM((k, tn), BF16))
    for arr, bs, im in extras:
        in_specs.append(pl.BlockSpec(bs, im))
        args.append(arr)
    n_tiles = n_cols // tn
    out_shape = [jax.ShapeDtypeStruct((m, n_tiles * out_tn), dt) for dt in out_dtypes]
    out_specs = [pl.BlockSpec((tm, out_tn), lambda j, i: (i, j)) for _ in out_dtypes]
    kern = functools.partial(_mm_kernel, a_counts=tuple(a_counts), n_g=n_g, n_x=len(extras), n_o=len(out_dtypes),
                             prologue=prologue, epilogue=epilogue, n_prompt_tiles=npt)
    outs = pl.pallas_call(
        kern, grid=(n_tiles, m // tm), in_specs=in_specs, out_specs=out_specs, out_shape=out_shape,
        scratch_shapes=scratch, compiler_params=_params("arbitrary", "arbitrary"))(*args)
    return outs


def _ep_plain(accs, xs):
    return accs[0]


def _ep_bias(accs, xs):
    return accs[0] + xs[0]


def _ep_residual(accs, xs):
    return xs[0] + xs[1] * accs[0]


def _make_ep_residual_pair(n_prompt_tiles):
    def ep(accs, xs):
        x = jnp.where(pl.program_id(1) < n_prompt_tiles, xs[0], xs[1])
        return x + xs[2] * accs[0]
    return ep


def _ep_swiglu(accs, xs):
    g = accs[0]
    return (g * jax.nn.sigmoid(g)) * accs[1]


def _ep_rmsnorm(accs, xs):
    acc = accs[0]
    y = acc * lax.rsqrt(jnp.mean(acc * acc, axis=-1, keepdims=True) + EPS) * xs[0]
    return y, y


def _slice_cols_kernel(x_ref, o_ref, *, c0, width):
    v = x_ref[:, c0:c0 + width]
    pad = o_ref.shape[1] - width
    if pad:
        v = jnp.concatenate([v, jnp.zeros((v.shape[0], pad), v.dtype)], axis=1)
    o_ref[...] = v


def slice_cols(x, c0, width, out_width, tr=256):
    rows, full = x.shape
    return pl.pallas_call(
        functools.partial(_slice_cols_kernel, c0=c0, width=width), grid=(rows // tr,),
        in_specs=[pl.BlockSpec((tr, full), lambda i: (i, 0))],
        out_specs=pl.BlockSpec((tr, out_width), lambda i: (i, 0)),
        out_shape=jax.ShapeDtypeStruct((rows, out_width), x.dtype),
        compiler_params=_params("arbitrary"))(x)


def _silu(x):
    return x * jax.nn.sigmoid(x)


def _modulate_kernel(*refs, n_prompt_tiles):
    g_ref, sh_ref, sc_ref, o_ref = refs[-4:]
    x = _pick_rows(refs[:-4], n_prompt_tiles, pl.program_id(0))
    y = x * lax.rsqrt(jnp.mean(x * x, axis=-1, keepdims=True) + EPS) * g_ref[...]
    o_ref[...] = (y * (1.0 + sc_ref[...]) + sh_ref[...]).astype(o_ref.dtype)


def _mod_spec(tok, chunk, width, col_of):
    def im(*g):
        i, j = col_of(*g)
        return (tok.mod_row(i) * N_MOD + chunk, 0, j)
    return ((None, 1, width), im)


class _Tokens:
    def __init__(self, n_p, t_p, n_s, t_s):
        self.n_p, self.t_p, self.n_s, self.t_s = n_p, t_p, n_s, t_s
        self.m = n_p + n_s

    def with_tile(self, tm):
        assert self.n_p % tm == 0 and self.t_s % tm == 0
        t = _Tokens(self.n_p, self.t_p, self.n_s, self.t_s)
        t.tm = tm
        return t

    def mod_row(self, i):
        start = i * self.tm
        return jnp.where(start < self.n_p, 0, 1 + (start - self.n_p) // self.t_s)


def modulate(x, g, mod, tok, shift_chunk, scale_chunk, tr=512):
    m, d = tok.m, g.shape[1]
    tk = tok.with_tile(tr)
    npt = tok.n_p // tr
    sh_bs, sh_im = _mod_spec(tk, shift_chunk, d, lambda i: (i, 0))
    sc_bs, sc_im = _mod_spec(tk, scale_chunk, d, lambda i: (i, 0))
    x_specs, x_args = _pair_specs(x, (tr, d), npt, lambda i: (i, 0))
    return pl.pallas_call(
        functools.partial(_modulate_kernel, n_prompt_tiles=npt), grid=(m // tr,),
        in_specs=x_specs + [pl.BlockSpec((1, d), lambda i: (0, 0)),
                            pl.BlockSpec(sh_bs, sh_im), pl.BlockSpec(sc_bs, sc_im)],
        out_specs=pl.BlockSpec((tr, d), lambda i: (i, 0)),
        out_shape=jax.ShapeDtypeStruct((m, d), BF16),
        compiler_params=_params("arbitrary"))(*x_args, g, mod, mod)


MOE_TILE = 512
MOE_ALIGN = 16
CHUNK_BITS = tuple(MOE_TILE >> s for s in range(6))


def _moe_buf_rows(n_experts):
    rows = TOP_K * MOE_TILE + n_experts * (MOE_ALIGN - 1)
    return -(-rows // LANES) * LANES


def _router_modulate_kernel(x_ref, g_ref, sh_ref, sc_ref, wr_ref, o_ref, route_ref, cnt_ref, *, n_experts):
    x = x_ref[...]
    y = x * lax.rsqrt(jnp.mean(x * x, axis=-1, keepdims=True) + EPS) * g_ref[...]
    h = y * (1.0 + sc_ref[...]) + sh_ref[...]
    o_ref[...] = h.astype(o_ref.dtype)
    w = wr_ref[...]
    h1 = h.astype(BF16)
    h2 = (h - h1.astype(F32)).astype(BF16)
    w1 = w.astype(BF16)
    w2 = (w - w1.astype(F32)).astype(BF16)
    logits = (jnp.dot(h1, w1, preferred_element_type=F32) + jnp.dot(h1, w2, preferred_element_type=F32)
              + jnp.dot(h2, w1, preferred_element_type=F32))
    tr = logits.shape[0]
    lane = lax.broadcasted_iota(jnp.int32, logits.shape, 1).astype(F32)
    neg = jnp.float32(-jnp.inf)
    lg = jnp.where(lane < n_experts, logits, neg)
    m1 = jnp.max(lg, axis=-1, keepdims=True)
    i1 = jnp.min(jnp.where(lg == m1, lane, float(LANES)), axis=-1, keepdims=True)
    lg2 = jnp.where(lane == i1, neg, lg)
    m2 = jnp.max(lg2, axis=-1, keepdims=True)
    i2 = jnp.min(jnp.where(lg2 == m2, lane, float(LANES)), axis=-1, keepdims=True)
    e2 = jnp.exp(m2 - m1)
    p1 = 1.0 / (1.0 + e2)
    p2 = e2 / (1.0 + e2)
    ind = jnp.where((lane == i1) | (lane == i2), 1.0, 0.0)
    r = lax.broadcasted_iota(jnp.int32, (tr, tr), 0)
    c = lax.broadcasted_iota(jnp.int32, (tr, tr), 1)
    rank = jnp.dot(jnp.where(c < r, 1.0, 0.0).astype(BF16), ind.astype(BF16), preferred_element_type=F32)
    cnt = jnp.sum(ind, axis=0, keepdims=True)
    cpad = jnp.floor((cnt + (MOE_ALIGN - 1.0)) * (1.0 / MOE_ALIGN)) * MOE_ALIGN
    lane1 = lane[0:1, :]
    off = jnp.zeros_like(cpad)
    for e in range(n_experts - 1):
        off = off + jnp.where(lane1 > e, cpad[:, e:e + 1], 0.0)
    pos = off + rank
    lp1 = jnp.sum(jnp.where(lane == i1, pos, 0.0), axis=-1, keepdims=True)
    lp2 = jnp.sum(jnp.where(lane == i2, pos, 0.0), axis=-1, keepdims=True)
    route_ref[...] = (jnp.where(lane == 0, lp1, 0.0) + jnp.where(lane == 1, lp2, 0.0)
                      + jnp.where(lane == 2, p1, 0.0) + jnp.where(lane == 3, p2, 0.0))
    cnt_ref[...] = jnp.broadcast_to(cpad, cnt_ref.shape)


def router_modulate(x, g, mod, tok, shift_chunk, scale_chunk, w_router_padded, n_experts):
    m, d = x.shape
    tr = MOE_TILE
    tk = tok.with_tile(tr)
    sh_bs, sh_im = _mod_spec(tk, shift_chunk, d, lambda i: (i, 0))
    sc_bs, sc_im = _mod_spec(tk, scale_chunk, d, lambda i: (i, 0))
    h, route, cnt = pl.pallas_call(
        functools.partial(_router_modulate_kernel, n_experts=n_experts), grid=(m // tr,),
        in_specs=[pl.BlockSpec((tr, d), lambda i: (i, 0)), pl.BlockSpec((1, d), lambda i: (0, 0)),
                  pl.BlockSpec(sh_bs, sh_im), pl.BlockSpec(sc_bs, sc_im),
                  pl.BlockSpec((d, LANES), lambda i: (0, 0))],
        out_specs=[pl.BlockSpec((tr, d), lambda i: (i, 0)), pl.BlockSpec((tr, LANES), lambda i: (i, 0)),
                   pl.BlockSpec((8, LANES), lambda i: (i, 0))],
        out_shape=[jax.ShapeDtypeStruct((m, d), BF16), jax.ShapeDtypeStruct((m, LANES), F32),
                   jax.ShapeDtypeStruct((m // tr * 8, LANES), F32)],
        compiler_params=_params("arbitrary"))(x, g, mod, mod, w_router_padded)
    return h, route, cnt[::8, :n_experts].astype(jnp.int32)


def _chunk_copies(t, src_ref, dst_ref, len_ref, n_experts, make_copy, action):
    for e in range(n_experts):
        n = len_ref[t * n_experts + e]
        s = src_ref[t * n_experts + e]
        d = dst_ref[t * n_experts + e]
        off = jnp.int32(0)
        for bit in CHUNK_BITS:
            @pl.when((n & bit) != 0)
            def _(off=off, bit=bit, s=s, d=d):
                cp = make_copy(pl.multiple_of(s + off, MOE_ALIGN), pl.multiple_of(d + off, MOE_ALIGN), bit)
                if action == "start":
                    cp.start()
                else:
                    cp.wait()
            off = off + (n & bit)


def _moe_scatter_kernel(src_ref, dst_ref, len_ref, h_ref, route_ref, init_ref, xs_ref, buf, sem, *, n_experts):
    del init_ref
    t = pl.program_id(0)
    rows = buf.shape[0]
    lane = lax.broadcasted_iota(jnp.int32, (h_ref.shape[0], rows), 1).astype(F32)
    sel = jnp.where((lane == route_ref[:, 0:1]) | (lane == route_ref[:, 1:2]), 1.0, 0.0).astype(BF16)
    buf[...] = lax.dot_general(sel, h_ref[...], (((0,), (0,)), ((), ())),
                               preferred_element_type=F32).astype(buf.dtype)

    def make_copy(s, d, n):
        return pltpu.make_async_copy(buf.at[pl.ds(s, n), :], xs_ref.at[pl.ds(d, n), :], sem)

    _chunk_copies(t, src_ref, dst_ref, len_ref, n_experts, make_copy, "start")
    _chunk_copies(t, src_ref, dst_ref, len_ref, n_experts, make_copy, "wait")


def moe_scatter(h, route, src, dst, ln, n_rows, n_experts):
    m, d = h.shape
    tr = MOE_TILE
    rows = _moe_buf_rows(n_experts)
    init = jnp.zeros((n_rows, d), BF16)
    grid_spec = pltpu.PrefetchScalarGridSpec(
        num_scalar_prefetch=3, grid=(m // tr,),
        in_specs=[pl.BlockSpec((tr, d), lambda i, *_: (i, 0)), pl.BlockSpec((tr, LANES), lambda i, *_: (i, 0)),
                  pl.BlockSpec(memory_space=pl.ANY)],
        out_specs=pl.BlockSpec(memory_space=pl.ANY),
        scratch_shapes=[pltpu.VMEM((rows, d), BF16), pltpu.SemaphoreType.DMA(())])
    return pl.pallas_call(
        functools.partial(_moe_scatter_kernel, n_experts=n_experts), grid_spec=grid_spec,
        out_shape=jax.ShapeDtypeStruct((n_rows, d), BF16), input_output_aliases={5: 0},
        compiler_params=_params("arbitrary"))(src, dst, ln, h, route, init)


GMM_ROW_STEP = 128


def _gmm_kernel(te_ref, first_ref, nv_ref, na_ref, a_ref, *refs, n_g, epilogue):
    w_refs = refs[:n_g]
    o_ref = refs[n_g]
    wb_refs = refs[n_g + 1:]
    j = pl.program_id(1)
    tm = o_ref.shape[0]

    @pl.when(first_ref[j] == 1)
    def _():
        for w, wb in zip(w_refs, wb_refs):
            wb[...] = w[...].astype(BF16)

    nv = nv_ref[j]
    for r in range(GMM_ROW_STEP, tm + 1, GMM_ROW_STEP):
        @pl.when((nv > r - GMM_ROW_STEP) & (nv <= r))
        def _(r=r):
            a = a_ref[:r, :]
            accs = [jnp.dot(a, wb[...], preferred_element_type=F32) for wb in wb_refs]
            o_ref[:r, :] = epilogue(accs, []).astype(o_ref.dtype)
            if r < tm:
                o_ref[r:, :] = jnp.zeros((tm - r, o_ref.shape[1]), o_ref.dtype)

    @pl.when(nv == 0)
    def _():
        o_ref[...] = jnp.zeros_like(o_ref)


def grouped_matmul(a, ws, plan, *, tn, epilogue, w_buffers=2):
    te, first, nv, na = plan
    rows, k = a.shape
    n = ws[0].shape[2]
    tm = MOE_TILE

    def row_map(jn, j, te, first, nv, na):
        return (jnp.minimum(j, na[0] - 1), 0)

    def w_map(jn, j, te, first, nv, na):
        return (te[j], 0, jn)

    def out_map(jn, j, te, first, nv, na):
        return (j, jn)

    w_kwargs = {} if w_buffers == 2 else {"pipeline_mode": pl.Buffered(w_buffers)}
    grid_spec = pltpu.PrefetchScalarGridSpec(
        num_scalar_prefetch=4, grid=(n // tn, rows // tm),
        in_specs=[pl.BlockSpec((tm, k), row_map)] + [pl.BlockSpec((None, k, tn), w_map, **w_kwargs) for _ in ws],
        out_specs=pl.BlockSpec((tm, tn), out_map),
        scratch_shapes=[pltpu.VMEM((k, tn), BF16) for _ in ws])
    return pl.pallas_call(
        functools.partial(_gmm_kernel, n_g=len(ws), epilogue=epilogue), grid_spec=grid_spec,
        out_shape=jax.ShapeDtypeStruct((rows, n), BF16),
        compiler_params=_params("arbitrary", "arbitrary"))(te, first, nv, na, a, *ws)


def _moe_combine_kernel(src_ref, dst_ref, len_ref, x_ref, route_ref, gate_ref, gf_ref, ys_ref, op_ref, os_ref,
                        buf, sem, *, n_experts, n_prompt_tiles):
    t = pl.program_id(0)

    @pl.when(t == 0)
    def _():
        buf[...] = jnp.zeros_like(buf)

    def make_copy(s, d, n):
        return pltpu.make_async_copy(ys_ref.at[pl.ds(d, n), :], buf.at[pl.ds(s, n), :], sem)

    _chunk_copies(t, src_ref, dst_ref, len_ref, n_experts, make_copy, "start")
    _chunk_copies(t, src_ref, dst_ref, len_ref, n_experts, make_copy, "wait")
    rows = buf.shape[0]
    lane = lax.broadcasted_iota(jnp.int32, (x_ref.shape[0], rows), 1).astype(F32)
    y = buf[...]
    s1 = jnp.where(lane == route_ref[:, 0:1], 1.0, 0.0).astype(BF16)
    s2 = jnp.where(lane == route_ref[:, 1:2], 1.0, 0.0).astype(BF16)
    f = (route_ref[:, 2:3] * jnp.dot(s1, y, preferred_element_type=F32)
         + route_ref[:, 3:4] * jnp.dot(s2, y, preferred_element_type=F32))
    x = x_ref[...] + gate_ref[...] * f
    y = x * lax.rsqrt(jnp.mean(x * x, axis=-1, keepdims=True) + EPS) * gf_ref[...]

    @pl.when(t < n_prompt_tiles)
    def _():
        op_ref[...] = y

    @pl.when(t >= n_prompt_tiles)
    def _():
        os_ref[...] = y


def moe_combine(x, route, mod, tok, gate_chunk, g_final, ys, src, dst, ln, n_experts):
    m, d = x.shape
    tr = MOE_TILE
    rows = _moe_buf_rows(n_experts)
    tk = tok.with_tile(tr)
    npt = tok.n_p // tr
    g_bs, g_im = _mod_spec(tk, gate_chunk, d, lambda i, *_: (i, 0))
    grid_spec = pltpu.PrefetchScalarGridSpec(
        num_scalar_prefetch=3, grid=(m // tr,),
        in_specs=[pl.BlockSpec((tr, d), lambda i, *_: (i, 0)), pl.BlockSpec((tr, LANES), lambda i, *_: (i, 0)),
                  pl.BlockSpec(g_bs, g_im), pl.BlockSpec((1, d), lambda i, *_: (0, 0)),
                  pl.BlockSpec(memory_space=pl.ANY)],
        out_specs=[pl.BlockSpec((tr, d), lambda i, *_: (jnp.minimum(i, npt - 1), 0)),
                   pl.BlockSpec((tr, d), lambda i, *_: (jnp.maximum(i - npt, 0), 0))],
        scratch_shapes=[pltpu.VMEM((rows, d), BF16), pltpu.SemaphoreType.DMA(())])
    return pl.pallas_call(
        functools.partial(_moe_combine_kernel, n_experts=n_experts, n_prompt_tiles=npt), grid_spec=grid_spec,
        out_shape=[jax.ShapeDtypeStruct((tok.n_p, d), F32), jax.ShapeDtypeStruct((tok.n_s, d), F32)],
        compiler_params=_params("arbitrary"))(src, dst, ln, x, route, mod, g_final, ys)


def _moe_plan(cpad, n_tiles_sorted):
    n_t, n_e = cpad.shape
    tot = jnp.sum(cpad, axis=0)
    region = (tot + MOE_TILE - 1) // MOE_TILE * MOE_TILE
    g_end = jnp.cumsum(region)
    dst = (g_end - region)[None, :] + jnp.cumsum(cpad, axis=0) - cpad
    src = jnp.cumsum(cpad, axis=1) - cpad
    start = jnp.arange(n_tiles_sorted, dtype=jnp.int32) * MOE_TILE
    active = start < g_end[-1]
    te = jnp.sum((start[:, None] >= g_end[None, :]).astype(jnp.int32), axis=1)
    te = jnp.where(active, te, jnp.max(jnp.where(active, te, 0)))
    first = active & jnp.concatenate([jnp.ones((1,), bool), te[1:] != te[:-1]])
    nv = jnp.where(active, jnp.clip(tot[te] - (start - (g_end - region)[te]), 0, MOE_TILE), 0)
    na = (g_end[-1:] // MOE_TILE).astype(jnp.int32)
    i32 = lambda v: v.reshape(-1).astype(jnp.int32)
    return (i32(src), i32(dst), i32(cpad)), (i32(te), i32(first), i32(nv), na)


def _rmsnorm_kernel(x_ref, g_ref, o_ref):
    x = x_ref[...]
    o_ref[...] = x * lax.rsqrt(jnp.mean(x * x, axis=-1, keepdims=True) + EPS) * g_ref[...]


def rmsnorm_rows(x, g, tr=512):
    m, d = x.shape
    return pl.pallas_call(
        _rmsnorm_kernel, grid=(m // tr,),
        in_specs=[pl.BlockSpec((tr, d), lambda i: (i, 0)), pl.BlockSpec((1, d), lambda i: (0, 0))],
        out_specs=pl.BlockSpec((tr, d), lambda i: (i, 0)),
        out_shape=jax.ShapeDtypeStruct((m, d), F32),
        compiler_params=_params("arbitrary"))(x, g)


def _make_conv_epilogue(tok, tm):
    def ep(accs, xs):
        ux, ub, uc = accs
        cw = xs[0]
        p = uc * ux
        i = pl.program_id(1)
        r = lax.broadcasted_iota(jnp.int32, p.shape, 0)
        is_p = i * tm < tok.n_p
        t = jnp.where(is_p, r % tok.t_p, r % tok.t_s)
        t_len = jnp.where(is_p, tok.t_p, tok.t_s)
        prev = jnp.where(t == 0, 0.0, pltpu.roll(p, 1, axis=0))
        nxt = jnp.where(t == t_len - 1, 0.0, pltpu.roll(p, tm - 1, axis=0))
        return ub * (prev * cw[0:1, :] + p * cw[1:2, :] + nxt * cw[2:3, :])
    return ep


def _swap16(x):
    n = x.shape[-1]
    lane = lax.broadcasted_iota(jnp.int32, x.shape, x.ndim - 1)
    fwd = pltpu.roll(x, n - 16, axis=x.ndim - 1)
    bwd = pltpu.roll(x, 16, axis=x.ndim - 1)
    return jnp.where((lane // 16) % 2 == 0, fwd, bwd)


def _attn_kernel(*refs, n_heads, with_ctx, scale):
    if with_ctx:
        (qn_ref, qr_ref, kv_ref, kr_ref, ckv_ref, ckr_ref, cq_ref, sq_ref, ck_ref, sk_ref, o_ref) = refs
    else:
        qn_ref, qr_ref, kv_ref, kr_ref, o_ref = refs
    qr = qr_ref[...]
    kr = kr_ref[...][:, :QK_ROPE]
    if with_ctx:
        qr = qr * cq_ref[...] + _swap16(qr) * sq_ref[...]
        kr128 = kr_ref[...]
        kr = (kr128 * ck_ref[...] + _swap16(kr128) * sk_ref[...])[:, :QK_ROPE]
    qr = qr.astype(BF16)
    kr = kr.astype(BF16)
    if with_ctx:
        ckr = ckr_ref[...].astype(BF16)
    dn = (((1,), (1,)), ((), ()))
    for h in range(n_heads):
        q = jnp.concatenate([qn_ref[:, h * QK_NOPE:(h + 1) * QK_NOPE],
                             qr[:, h * QK_ROPE:(h + 1) * QK_ROPE]], axis=1)
        c0 = h * (QK_NOPE + V_HEAD)
        k = jnp.concatenate([kv_ref[:, c0:c0 + QK_NOPE], kr], axis=1)
        s = lax.dot_general(q, k, dn, preferred_element_type=F32) * scale
        mx = jnp.max(s, axis=-1, keepdims=True)
        if with_ctx:
            kc = jnp.concatenate([ckv_ref[:, c0:c0 + QK_NOPE], ckr], axis=1)
            s2 = lax.dot_general(q, kc, dn, preferred_element_type=F32) * scale
            mx = jnp.maximum(mx, jnp.max(s2, axis=-1, keepdims=True))
        p = jnp.exp(s - mx)
        den = jnp.sum(p, axis=-1, keepdims=True)
        o = jnp.dot(p.astype(BF16), kv_ref[:, c0 + QK_NOPE:c0 + QK_NOPE + V_HEAD],
                    preferred_element_type=F32)
        if with_ctx:
            p2 = jnp.exp(s2 - mx)
            den = den + jnp.sum(p2, axis=-1, keepdims=True)
            o = o + jnp.dot(p2.astype(BF16), ckv_ref[:, c0 + QK_NOPE:c0 + QK_NOPE + V_HEAD],
                            preferred_element_type=F32)
        o_ref[:, h * V_HEAD:(h + 1) * V_HEAD] = (o / den).astype(o_ref.dtype)


def attention(qn, qr, kv, kr, *, row0, n_batch, t, tq, ctx=None):
    h = MLA_HEADS
    nq = t // tq
    qb0 = row0 // tq
    kb0 = row0 // t
    scale = float(QK_NOPE + QK_ROPE) ** -0.5
    in_specs = [pl.BlockSpec((tq, h * QK_NOPE), lambda b, i: (qb0 + b * nq + i, 0)),
                pl.BlockSpec((tq, h * QK_ROPE), lambda b, i: (qb0 + b * nq + i, 0)),
                pl.BlockSpec((t, h * (QK_NOPE + V_HEAD)), lambda b, i: (kb0 + b, 0)),
                pl.BlockSpec((t, LANES), lambda b, i: (kb0 + b, 0))]
    args = [qn, qr, kv, kr]
    if ctx is not None:
        ckv, ckr, past, (cq, sq, ck, sk) = ctx
        in_specs += [pl.BlockSpec((past, h * (QK_NOPE + V_HEAD)), lambda b, i: (b, 0)),
                     pl.BlockSpec((past, QK_ROPE), lambda b, i: (b, 0)),
                     pl.BlockSpec((tq, h * QK_ROPE), lambda b, i: (i, 0)),
                     pl.BlockSpec((tq, h * QK_ROPE), lambda b, i: (i, 0)),
                     pl.BlockSpec((t, LANES), lambda b, i: (0, 0)),
                     pl.BlockSpec((t, LANES), lambda b, i: (0, 0))]
        args += [ckv, ckr, cq, sq, ck, sk]
    return pl.pallas_call(
        functools.partial(_attn_kernel, n_heads=h, with_ctx=ctx is not None, scale=scale),
        grid=(n_batch, nq), in_specs=in_specs,
        out_specs=pl.BlockSpec((tq, h * V_HEAD), lambda b, i: (b * nq + i, 0)),
        out_shape=jax.ShapeDtypeStruct((n_batch * t, h * V_HEAD), BF16),
        compiler_params=_params("arbitrary", "arbitrary"))(*args)


def _rope_tables(t):
    rows = t // GRID_W
    row = jnp.repeat(jnp.arange(rows, dtype=F32), GRID_W)
    col = jnp.tile(jnp.arange(GRID_W, dtype=F32), rows)
    half = QK_ROPE // 2
    inv_freq = ROPE_BASE ** (-jnp.arange(0, half, 2, dtype=F32) / half)
    ang_r = row[:, None] * inv_freq
    ang_c = col[:, None] * inv_freq
    cr, sr, cc, sc = jnp.cos(ang_r), jnp.sin(ang_r), jnp.cos(ang_c), jnp.sin(ang_c)
    cos = jnp.concatenate([cr, cr, cc, cc], axis=-1)
    sin = jnp.concatenate([-sr, sr, -sc, sc], axis=-1)
    return cos, sin


def _log_sigmoid(x):
    return jnp.minimum(x, 0.0) - jnp.log(1.0 + jnp.exp(-jnp.abs(x)))


def _mlstm_kernel(*refs, chunk, n_heads, dk, dv, has_init):
    if has_init:
        (q_ref, k_ref, v_ref, g_ref, gt_ref, bg_ref, bgt_ref, c0_ref, n0_ref, m0_ref,
         h_ref, c_ref, n_ref, m_ref) = refs
    else:
        q_ref, k_ref, v_ref, g_ref, gt_ref, bg_ref, bgt_ref, h_ref, c_ref, n_ref, m_ref = refs
    d = pl.program_id(1)

    @pl.when(pl.program_id(2) == 0)
    def _():
        if has_init:
            c_ref[...] = c0_ref[...]
            n_ref[...] = n0_ref[...]
            m_ref[...] = m0_ref[...]
        else:
            c_ref[...] = jnp.zeros_like(c_ref)
            n_ref[...] = jnp.zeros_like(n_ref)
            m_ref[...] = jnp.zeros_like(m_ref)

    L = chunk
    row = lax.broadcasted_iota(jnp.int32, (L, L), 0)
    col = lax.broadcasted_iota(jnp.int32, (L, L), 1)
    sgn = jnp.where(d == 0, 1, -1)
    mask = (col - row) * sgn <= 0
    mask_t = (row - col) * sgn <= 0
    neg = jnp.float32(-jnp.inf)
    g = g_ref[...] + bg_ref[...]
    gt = gt_ref[...] + bgt_ref[...]
    dn_t = (((0,), (0,)), ((), ()))
    for h in range(n_heads):
        i_col = g[:, h:h + 1]
        f_col = _log_sigmoid(g[:, n_heads + h:n_heads + h + 1])
        i_row = gt[h:h + 1, :]
        f_row = _log_sigmoid(gt[n_heads + h:n_heads + h + 1, :])
        b_col = jnp.sum(jnp.where(mask, f_row, 0.0), axis=1, keepdims=True)
        b_row = jnp.sum(jnp.where(mask_t, f_col, 0.0), axis=0, keepdims=True)
        m_old = m_ref[:, h:h + 1]
        dmat = jnp.where(mask, b_col - b_row + i_row, neg)
        a_col = b_col + m_old
        mrow = jnp.maximum(a_col, jnp.max(dmat, axis=1, keepdims=True))
        w_intra = jnp.exp(dmat - mrow)
        w_inter = jnp.exp(a_col - mrow)
        qh = q_ref[:, h * dk:(h + 1) * dk]
        kh = k_ref[:, h * dk:(h + 1) * dk] * (dk ** -0.5)
        vh = v_ref[:, h * dv:(h + 1) * dv]
        c_old = c_ref[h]
        n_old = n_ref[h:h + 1, :]
        s = lax.dot_general(qh, kh, (((1,), (1,)), ((), ())), preferred_element_type=F32) * w_intra
        num = (w_inter * jnp.dot(qh, c_old.astype(BF16), preferred_element_type=F32)
               + jnp.dot(s.astype(BF16), vh, preferred_element_type=F32))
        qn = jnp.sum(qh.astype(F32) * n_old, axis=1, keepdims=True)
        den = w_inter * qn + jnp.sum(s, axis=1, keepdims=True)
        h_ref[:, h * dv:(h + 1) * dv] = num / jnp.maximum(jnp.abs(den), jnp.exp(-mrow))
        bl = jnp.sum(f_row, axis=1, keepdims=True)
        g_row = bl - b_row + i_row
        g_col = bl - b_col + i_col
        m_new = jnp.maximum(bl + m_old, jnp.max(g_row, axis=1, keepdims=True))
        wg_col = jnp.exp(g_col - m_new)
        decay = jnp.exp(bl + m_old - m_new)
        kw = kh.astype(F32) * wg_col
        c_ref[h] = decay * c_old + lax.dot_general(kw.astype(BF16), vh, dn_t, preferred_element_type=F32)
        n_ref[h:h + 1, :] = decay * n_old + jnp.sum(kw, axis=0, keepdims=True)
        m_ref[:, h:h + 1] = m_new


def mlstm(qkv, gates, gates_t, bg, bg_t, *, row0, n_batch, t, chunk, init=None):
    nh = ML_HEADS
    hv = qkv.shape[1] // 2
    dv = hv // nh
    dk = dv // 2
    hk = nh * dk
    nc = t // chunk
    rb0 = row0 // chunk

    def blk(b, d, c):
        return rb0 + b * nc + jnp.where(d == 0, c, nc - 1 - c)

    in_specs = [pl.BlockSpec((chunk, hk), lambda b, d, c: (blk(b, d, c), 0)),
                pl.BlockSpec((chunk, hk), lambda b, d, c: (blk(b, d, c), 1)),
                pl.BlockSpec((chunk, hv), lambda b, d, c: (blk(b, d, c), 1)),
                pl.BlockSpec((None, chunk, 2 * nh), lambda b, d, c: (d, blk(b, d, c), 0)),
                pl.BlockSpec((None, 2 * nh, chunk), lambda b, d, c: (d, 0, blk(b, d, c))),
                pl.BlockSpec((None, 1, 2 * nh), lambda b, d, c: (d, 0, 0)),
                pl.BlockSpec((None, 2 * nh, 1), lambda b, d, c: (d, 0, 0))]
    args = [qkv, qkv, qkv, gates, gates_t, bg, bg_t]
    if init is not None:
        in_specs += [pl.BlockSpec((None, None, nh, dk, dv), lambda b, d, c: (b, d, 0, 0, 0)),
                     pl.BlockSpec((None, None, nh, dk), lambda b, d, c: (b, d, 0, 0)),
                     pl.BlockSpec((None, None, 1, nh), lambda b, d, c: (b, d, 0, 0))]
        args += list(init)
    out_specs = [pl.BlockSpec((None, chunk, hv), lambda b, d, c: (d, blk(b, d, c) - rb0, 0)),
                 pl.BlockSpec((None, None, nh, dk, dv), lambda b, d, c: (b, d, 0, 0, 0)),
                 pl.BlockSpec((None, None, nh, dk), lambda b, d, c: (b, d, 0, 0)),
                 pl.BlockSpec((None, None, 1, nh), lambda b, d, c: (b, d, 0, 0))]
    out_shape = [jax.ShapeDtypeStruct((2, n_batch * t, hv), F32),
                 jax.ShapeDtypeStruct((n_batch, 2, nh, dk, dv), F32),
                 jax.ShapeDtypeStruct((n_batch, 2, nh, dk), F32),
                 jax.ShapeDtypeStruct((n_batch, 2, 1, nh), F32)]
    return pl.pallas_call(
        functools.partial(_mlstm_kernel, chunk=chunk, n_heads=nh, dk=dk, dv=dv, has_init=init is not None),
        grid=(n_batch, 2, nc), in_specs=in_specs, out_specs=out_specs, out_shape=out_shape,
        compiler_params=_params("arbitrary", "arbitrary", "arbitrary"))(*args)


def _mlstm_out_kernel(hfp_ref, hbp_ref, hfs_ref, hbs_ref, o_ref, g_ref, y_ref, *, n_heads, dv, n_prompt_tiles):
    is_p = pl.program_id(0) < n_prompt_tiles
    for h in range(n_heads):
        sl = slice(h * dv, (h + 1) * dv)
        x = jnp.where(is_p, hfp_ref[:, sl] + hbp_ref[:, sl], hfs_ref[:, sl] + hbs_ref[:, sl])
        ht = x * lax.rsqrt(jnp.mean(x * x, axis=-1, keepdims=True) + EPS) * g_ref[:, sl]
        y_ref[:, sl] = (jax.nn.sigmoid(o_ref[:, sl]) * ht).astype(y_ref.dtype)


def mlstm_out(h_p, h_s, o, g_h, tr=256):
    hv = h_p.shape[2]
    npt = h_p.shape[1] // tr
    m = h_p.shape[1] + h_s.shape[1]

    def p_spec(d):
        return pl.BlockSpec((None, tr, hv), lambda i: (d, jnp.minimum(i, npt - 1), 0))

    def s_spec(d):
        return pl.BlockSpec((None, tr, hv), lambda i: (d, jnp.maximum(i - npt, 0), 0))

    return pl.pallas_call(
        functools.partial(_mlstm_out_kernel, n_heads=ML_HEADS, dv=hv // ML_HEADS, n_prompt_tiles=npt),
        grid=(m // tr,),
        in_specs=[p_spec(0), p_spec(1), s_spec(0), s_spec(1),
                  pl.BlockSpec((tr, hv), lambda i: (i, 0)),
                  pl.BlockSpec((1, hv), lambda i: (0, 0))],
        out_specs=pl.BlockSpec((tr, hv), lambda i: (i, 0)),
        out_shape=jax.ShapeDtypeStruct((m, hv), BF16),
        compiler_params=_params("arbitrary"))(h_p, h_p, h_s, h_s, o, g_h)


def _adaln(cond, w_stack, layer, b):
    rows, d = cond.shape
    n = w_stack.shape[2]
    tn = n // 8
    out, = matmul([(cond, None)], [[(w_stack, 0, 0, layer)]], [F32], n_cols=n, tm=rows, tn=tn, epilogue=_ep_bias,
                  extras=[(b.reshape(1, n), (1, tn), lambda j, i: (0, j))], prologue=_silu)
    return out.reshape(rows * N_MOD, 1, d)


def _residual_extras(x, mod, tok, tm, tn, chunk):
    tk = tok.with_tile(tm)
    bs, im = _mod_spec(tk, chunk, tn, lambda j, i: (i, j))
    specs, arrs = _pair_specs(x, (tm, tn), tok.n_p // tm, lambda j, i: (i, j))
    return [(a, s.block_shape, s.index_map) for a, s in zip(arrs, specs)] + [(mod, bs, im)]


def kernel(x_prompt, x_sample, c, cache_ckv, cache_krope, state_C, state_n, state_m, c_ctx, w_mod, b_mod, g_mix, g_ffn, w_in_a, g_q, g_kv, w_uq, w_ukv, conv_w, w_o_a, w_in_c, b_gates, g_h, w_o_c, w_ffn_gate, w_ffn_up, w_ffn_down, w_router, w_exp_gate, w_exp_up, w_exp_down, g_final):
    bp, tp, d = x_prompt.shape
    bs, ts, _ = x_sample.shape
    past = cache_ckv.shape[2]
    n_p, n_s = bp * tp, bs * ts
    tok = _Tokens(n_p, tp, n_s, ts)
    m = tok.m
    assert n_p % ts == 0 and ts % 512 == 0 and tp % 256 == 0
    x = (x_prompt.reshape(n_p, d), x_sample.reshape(n_s, d))

    n_cond = -(-(1 + bs) // 8) * 8
    cond = jnp.concatenate([c_ctx[None], c, jnp.zeros((n_cond - 1 - bs, d), F32)], axis=0)

    TM, TN = 512, 512
    TM2, TN2 = 1024, 1024

    mod = _adaln(cond, w_mod, 0, b_mod[0])
    h = modulate(x, g_mix[0].reshape(1, d), mod, tok, 0, 1)
    w_in = w_in_a[0]
    conv_dim = conv_w.shape[2]
    c_kr = Q_LORA + KV_LORA
    c_ux = c_kr + QK_ROPE
    gq_kv = jnp.concatenate([g_q[0], g_kv[0]])[None]
    lat_f32, lat_bf = matmul([(h, None)], [[(w_in, 0, 0)]], [F32, BF16], n_cols=c_kr, tm=TM2, tn=Q_LORA,
                             epilogue=_ep_rmsnorm, extras=[(gq_kv, (1, Q_LORA), lambda j, i: (0, j))])
    assert c_kr % LANES == 0
    kr, = matmul([(h, None)], [[(w_in, 0, c_kr // LANES)]], [F32], n_cols=LANES, tm=TM2, tn=LANES,
                 epilogue=_ep_plain)
    w_conv = slice_cols(w_in, c_ux, 3 * conv_dim, 3 * conv_dim)
    cb = 256
    ncb = conv_dim // cb
    conv, = matmul([(h, None)], [[(w_conv, 0, g * ncb)] for g in range(3)], [BF16], n_cols=conv_dim,
                   tm=ts, tn=cb, epilogue=_make_conv_epilogue(tok, ts),
                   extras=[(conv_w[0], (CONV_W, cb), lambda j, i: (0, j))])
    wq = w_uq[0].reshape(Q_LORA, MLA_HEADS, QK_NOPE + QK_ROPE)
    wq_n = wq[:, :, :QK_NOPE].reshape(Q_LORA, MLA_HEADS * QK_NOPE)
    wq_r = wq[:, :, QK_NOPE:].reshape(Q_LORA, MLA_HEADS * QK_ROPE)
    qn, = matmul([(lat_bf, (0, Q_LORA))], [[(wq_n, 0, 0)]], [BF16], n_cols=MLA_HEADS * QK_NOPE, tm=TM2, tn=TN2,
                 epilogue=_ep_plain)
    qr, = matmul([(lat_bf, (0, Q_LORA))], [[(wq_r, 0, 0)]], [F32], n_cols=MLA_HEADS * QK_ROPE, tm=TM2, tn=TN,
                 epilogue=_ep_plain)
    n_kv = MLA_HEADS * (QK_NOPE + V_HEAD)
    kv, = matmul([(lat_bf, (1, KV_LORA))], [[(w_ukv[0], 0, 0)]], [BF16], n_cols=n_kv, tm=TM2, tn=TN2,
                 epilogue=_ep_plain)
    ctx_kv, = matmul([(cache_ckv[:, 0].reshape(bs * past, KV_LORA), None)], [[(w_ukv[0], 0, 0)]], [BF16],
                     n_cols=n_kv, tm=TM, tn=TN2, epilogue=_ep_plain)
    ctx_kr = cache_krope[:, 0].reshape(bs * past, QK_ROPE)
    cos, sin = _rope_tables(ts)
    pad = ((0, 0), (0, LANES - QK_ROPE))
    tables = (jnp.tile(cos, (1, MLA_HEADS)), jnp.tile(sin, (1, MLA_HEADS)), jnp.pad(cos, pad), jnp.pad(sin, pad))
    att_p = attention(qn, qr, kv, kr, row0=0, n_batch=bp, t=tp, tq=tp)
    att_s = attention(qn, qr, kv, kr, row0=n_p, n_batch=bs, t=ts, tq=256, ctx=(ctx_kv, ctx_kr, past, tables))
    assert MLA_HEADS * V_HEAD == conv_dim
    x, = matmul([((att_p, att_s), None), (conv, None)], [[(w_o_a[0], 0, 0), (w_o_a[0], 1, 0)]], [F32], n_cols=d,
                tm=TM, tn=TN2, epilogue=_make_ep_residual_pair(n_p // TM), n_prompt_rows=n_p,
                extras=_residual_extras(x, mod, tok, TM, TN2, 2))
    h = modulate(x, g_ffn[0].reshape(1, d), mod, tok, 3, 4)
    d_ff = w_ffn_gate.shape[2]
    hid, = matmul([(h, None)], [[(w_ffn_gate[0], 0, 0)], [(w_ffn_up[0], 0, 0)]], [BF16], n_cols=d_ff, tm=TM,
                  tn=TN2, epilogue=_ep_swiglu)
    x, = matmul([(hid, None)], [[(w_ffn_down[0], 0, 0)]], [F32], n_cols=d, tm=TM, tn=TN, w_buffers=1,
                epilogue=_ep_residual, extras=_residual_extras(x, mod, tok, TM, TN, 5))
    ckv_p = lat_f32[:n_p, Q_LORA:].reshape(bp, 1, tp, KV_LORA)
    kr_p = kr[:n_p, :QK_ROPE].reshape(bp, 1, tp, QK_ROPE)

    mod = _adaln(cond, w_mod, 1, b_mod[1])
    h = modulate(x, g_mix[1].reshape(1, d), mod, tok, 0, 1)
    w_c = w_in_c[0]
    hv = d
    hk = hv // 2
    n_qkv = 2 * hk + hv
    qkv, = matmul([(h, None)], [[(w_c, 0, 0)]], [BF16], n_cols=n_qkv, tm=TM2, tn=TN2, epilogue=_ep_plain)
    o_gate, = matmul([(h, None)], [[(w_c, 0, n_qkv // TN2)]], [F32], n_cols=hv, tm=TM2, tn=TN2,
                     epilogue=_ep_plain)
    n_g = 4 * ML_HEADS
    w_g = slice_cols(w_c, n_qkv + hv, n_g, LANES)
    graw, = matmul([(h, None)], [[(w_g, 0, 0)]], [F32], n_cols=LANES, tm=TM2, tn=LANES, epilogue=_ep_plain)
    gates = graw[:, :n_g].reshape(m, 2, 2 * ML_HEADS).transpose(1, 0, 2)
    gates_t = gates.transpose(0, 2, 1)
    bg = b_gates[0].reshape(2, 1, 2 * ML_HEADS)
    bg_t = bg.transpose(0, 2, 1)
    chunk = 256
    h_p, c_p, n_p_state, m_p = mlstm(qkv, gates, gates_t, bg, bg_t, row0=0, n_batch=bp, t=tp, chunk=chunk)
    init = (state_C[:, 0], state_n[:, 0], state_m[:, 0].reshape(bs, 2, 1, ML_HEADS))
    h_s, _, _, _ = mlstm(qkv, gates, gates_t, bg, bg_t, row0=n_p, n_batch=bs, t=ts, chunk=chunk, init=init)
    y = mlstm_out(h_p, h_s, o_gate, g_h[0].reshape(1, hv))
    x, = matmul([(y, None)], [[(w_o_c[0], 0, 0)]], [F32], n_cols=d, tm=TM2, tn=TN2, epilogue=_ep_residual,
                extras=_residual_extras(x, mod, tok, TM2, TN2, 2))
    n_exp = w_router.shape[2]
    w_r = jnp.pad(w_router[0], ((0, 0), (0, LANES - n_exp)))
    h, route, cpad = router_modulate(x, g_ffn[1].reshape(1, d), mod, tok, 3, 4, w_r, n_exp)
    n_tt = m // MOE_TILE
    n_sorted = -(-(TOP_K * m + n_tt * n_exp * (MOE_ALIGN - 1) + n_exp * (MOE_TILE - 1)) // MOE_TILE)
    (src, dst, ln), plan = _moe_plan(cpad, n_sorted)
    xs = moe_scatter(h, route, src, dst, ln, n_sorted * MOE_TILE, n_exp)
    hid = grouped_matmul(xs, [w_exp_gate[0], w_exp_up[0]], plan, tn=TN2, epilogue=_ep_swiglu)
    ys = grouped_matmul(hid, [w_exp_down[0]], plan, tn=TN, epilogue=_ep_plain)
    y_p, y_s = moe_combine(x, route, mod, tok, 5, g_final.reshape(1, d), ys, src, dst, ln, n_exp)
    y_prompt = y_p.reshape(bp, tp, d)
    y_sample = y_s.reshape(bs, ts, d)
    new_c = c_p[:, None]
    new_n = n_p_state[:, None]
    new_m = m_p.reshape(bp, 1, 2, ML_HEADS)
    return (y_prompt, y_sample, ckv_p, kr_p, new_c, new_n, new_m)
```
